```python
import math
import jax, jax.numpy as jnp
from jax import lax
import numpy as np

D_MODEL = 1024
BATCH = 8
SEQ = 4096
DEPTH = 2
DEC_BATCH = 32
DEC_SEQ = 1
PAST_LEN = 16384
PAGE_SIZE = 128

N_MIXERS = 2
N_CONV_LAYERS = (DEPTH + 1) // 2
N_ATTN_LAYERS = DEPTH // 2
N_HEADS = 8
HEAD_DIM = D_MODEL // N_HEADS
ROT_DIM = HEAD_DIM // 4
ROPE_THETA = 500000.0
MOBA_BLOCK = 256
MOBA_TOPK = 3
Q_CHUNK = 128
CONV_WIDTH = 3
D_FF = -(-8 * D_MODEL // (3 * 256)) * 256
RMS_EPS = 1e-6

kernel_name = 'hybrid_shortconv_moba_step'


def rms_norm(x, g):
    xf = x.astype(jnp.float32)
    y = xf * lax.rsqrt(jnp.mean(xf * xf, axis=-1, keepdims=True) + RMS_EPS)
    return (y * g.astype(jnp.float32)).astype(x.dtype)


def rope_partial(x, pos):
    half = ROT_DIM // 2
    inv = jnp.power(jnp.float32(ROPE_THETA), -jnp.arange(half, dtype=jnp.float32) * 2.0 / ROT_DIM)
    ang = pos.astype(jnp.float32)[:, None] * inv[None, :]
    cos = jnp.cos(ang)[:, None, :]
    sin = jnp.sin(ang)[:, None, :]
    xr = x[..., :ROT_DIM].astype(jnp.float32)
    x1, x2 = xr[..., :half], xr[..., half:]
    rot = jnp.concatenate([x1 * cos - x2 * sin, x2 * cos + x1 * sin], axis=-1).astype(x.dtype)
    return jnp.concatenate([rot, x[..., ROT_DIM:]], axis=-1)


def conv_mixer(x, state, w_in, w_conv, w_out):
    bcx = x @ w_in
    b, c, xin = jnp.split(bcx, 3, axis=-1)
    u = c * xin
    u_ext = jnp.concatenate([state.astype(u.dtype), u], axis=1)
    conv = w_conv[0] * u_ext[:, :-2] + w_conv[1] * u_ext[:, 1:-1] + w_conv[2] * u_ext[:, 2:]
    return (b * conv) @ w_out, u_ext[:, -(CONV_WIDTH - 1):]


def swiglu(x, w_gate_up, w_down):
    g, u = jnp.split(x @ w_gate_up, 2, axis=-1)
    return (jax.nn.silu(g) * u) @ w_down


def qkv_rope(x, w_qkv, pos):
    n, t = x.shape[0], x.shape[1]
    qkv = (x @ w_qkv).reshape(n, t, 3, N_HEADS, HEAD_DIM)
    q, k, v = qkv[:, :, 0], qkv[:, :, 1], qkv[:, :, 2]
    return rope_partial(q, pos), rope_partial(k, pos), v


def moba_prompt(q, k, v):
    B, S = q.shape[0], q.shape[1]
    q = q.transpose(0, 2, 1, 3)
    k = k.transpose(0, 2, 1, 3)
    v = v.transpose(0, 2, 1, 3)
    n_full = S // MOBA_BLOCK
    n_blk = -(-S // MOBA_BLOCK)
    pad = n_blk * MOBA_BLOCK - S
    kp = jnp.pad(k, ((0, 0), (0, 0), (0, pad), (0, 0))).reshape(B, N_HEADS, n_blk, MOBA_BLOCK, HEAD_DIM)
    vp = jnp.pad(v, ((0, 0), (0, 0), (0, pad), (0, 0))).reshape(B, N_HEADS, n_blk, MOBA_BLOCK, HEAD_DIM)
    n_sel = min(MOBA_TOPK, n_full)
    q_blk = jnp.arange(S) // MOBA_BLOCK
    if n_sel > 0:
        kmean = jnp.mean(kp[:, :, :n_full].astype(jnp.float32), axis=3)
        gate = jnp.einsum('bhsd,bhnd->bhsn', q.astype(jnp.float32), kmean)
        past = jnp.arange(n_full)[None, :] < q_blk[:, None]
        gate = jnp.where(past, gate, -jnp.inf)
        _, sel = lax.top_k(gate, n_sel)
        sel_ok = jnp.arange(n_sel)[None, :] < q_blk[:, None]
    n_chunks = S // Q_CHUNK
    scale = HEAD_DIM ** -0.5
    h_idx = jnp.arange(N_HEADS)[:, None, None]

    def one_chunk(bc):
        bi = bc // n_chunks
        q0 = (bc % n_chunks) * Q_CHUNK
        qc = lax.dynamic_slice_in_dim(q[bi], q0, Q_CHUNK, axis=1)
        qpos = q0 + jnp.arange(Q_CHUNK)
        ob = q0 // MOBA_BLOCK
        k_own = kp[bi, :, ob]
        v_own = vp[bi, :, ob]
        kpos = ob * MOBA_BLOCK + jnp.arange(MOBA_BLOCK)
        s_own = jnp.einsum('hqd,hkd->hqk', qc, k_own).astype(jnp.float32) * scale
        s_own = jnp.where(kpos[None, None, :] <= qpos[None, :, None], s_own, -jnp.inf)
        if n_sel > 0:
            sc = lax.dynamic_slice_in_dim(sel[bi], q0, Q_CHUNK, axis=1)
            okc = lax.dynamic_slice_in_dim(sel_ok, q0, Q_CHUNK, axis=0)
            k_sel = kp[bi][h_idx, sc]
            v_sel = vp[bi][h_idx, sc]
            s_sel = jnp.einsum('hqd,hqnkd->hqnk', qc, k_sel).astype(jnp.float32) * scale
            s_sel = jnp.where(okc[None, :, :, None], s_sel, -jnp.inf).reshape(N_HEADS, Q_CHUNK, n_sel * MOBA_BLOCK)
            p = jax.nn.softmax(jnp.concatenate([s_sel, s_own], axis=-1), axis=-1).astype(v.dtype)
            n_s = n_sel * MOBA_BLOCK
            o = (jnp.einsum('hqk,hqkd->hqd', p[..., :n_s], v_sel.reshape(N_HEADS, Q_CHUNK, n_s, HEAD_DIM))
                 + jnp.einsum('hqk,hkd->hqd', p[..., n_s:], v_own))
        else:
            p = jax.nn.softmax(s_own, axis=-1).astype(v.dtype)
            o = jnp.einsum('hqk,hkd->hqd', p, v_own)
        return o

    out = lax.map(one_chunk, jnp.arange(B * n_chunks))
    out = out.reshape(B, n_chunks, N_HEADS, Q_CHUNK, HEAD_DIM).transpose(0, 1, 3, 2, 4)
    return out.reshape(B, S, N_HEADS * HEAD_DIM)


def moba_sample(q, kn, vn, cache_k, cache_v, a, page_table):
    DB, T = q.shape[0], q.shape[1]
    q = q.transpose(0, 2, 1, 3)
    kn = kn.transpose(0, 2, 1, 3)
    vn = vn.transpose(0, 2, 1, 3)
    n_pages = PAST_LEN // PAGE_SIZE
    ppb = MOBA_BLOCK // PAGE_SIZE
    own_blk = PAST_LEN // MOBA_BLOCK
    n_full = own_blk
    n_sel = min(MOBA_TOPK, n_full)
    first_own_page = own_blk * ppb
    n_own_cached = (n_pages - first_own_page) * PAGE_SIZE
    scale = HEAD_DIM ** -0.5
    pt_own = page_table[:, first_own_page:n_pages]
    k_oc = cache_k[a, pt_own].transpose(0, 2, 1, 3, 4).reshape(DB, N_HEADS, n_own_cached, HEAD_DIM)
    v_oc = cache_v[a, pt_own].transpose(0, 2, 1, 3, 4).reshape(DB, N_HEADS, n_own_cached, HEAD_DIM)
    k_own = jnp.concatenate([k_oc.astype(kn.dtype), kn], axis=2)
    v_own = jnp.concatenate([v_oc.astype(vn.dtype), vn], axis=2)
    s_own = jnp.einsum('bhqd,bhkd->bhqk', q, k_own).astype(jnp.float32) * scale
    allowed = jnp.arange(n_own_cached + T)[None, :] <= (n_own_cached + jnp.arange(T))[:, None]
    s_own = jnp.where(allowed[None, None], s_own, -jnp.inf)
    if n_sel > 0:
        kpages = cache_k[a, page_table[:, :n_full * ppb]]
        kmean = jnp.sum(kpages.astype(jnp.float32), axis=3).reshape(DB, n_full, ppb, N_HEADS, HEAD_DIM).sum(axis=2) / MOBA_BLOCK
        gate = jnp.einsum('bhqd,bnhd->bhqn', q.astype(jnp.float32), kmean)
        _, sel = lax.top_k(gate, n_sel)
        lp = sel[..., None] * ppb + jnp.arange(ppb)
        phys = page_table[jnp.arange(DB)[:, None, None, None, None], lp]
        hi = jnp.arange(N_HEADS)[None, :, None, None, None]
        n_s = n_sel * MOBA_BLOCK
        k_sel = cache_k[a, phys, hi].reshape(DB, N_HEADS, T, n_s, HEAD_DIM).astype(q.dtype)
        v_sel = cache_v[a, phys, hi].reshape(DB, N_HEADS, T, n_s, HEAD_DIM).astype(vn.dtype)
        s_sel = jnp.einsum('bhqd,bhqkd->bhqk', q, k_sel).astype(jnp.float32) * scale
        p = jax.nn.softmax(jnp.concatenate([s_sel, s_own], axis=-1), axis=-1).astype(vn.dtype)
        o = (jnp.einsum('bhqk,bhqkd->bhqd', p[..., :n_s], v_sel)
             + jnp.einsum('bhqk,bhkd->bhqd', p[..., n_s:], v_own))
    else:
        p = jax.nn.softmax(s_own, axis=-1).astype(vn.dtype)
        o = jnp.einsum('bhqk,bhkd->bhqd', p, v_own)
    return o.transpose(0, 2, 1, 3).reshape(DB, T, N_HEADS * HEAD_DIM)


def setup_inputs(seed: int = 0) -> dict:
    key = jax.random.key(seed)
    ks = jax.random.split(key, 18)
    n_pages = PAST_LEN // PAGE_SIZE
    n_pool = (DEC_BATCH * n_pages * 5 + 3) // 4
    f32 = jnp.float32
    nrm = lambda k, shp, s: jax.random.normal(k, shp, f32) * s
    perm = jax.random.permutation(ks[5], n_pool)[:DEC_BATCH * n_pages]
    return {
        'x_prompt': nrm(ks[0], (BATCH, SEQ, D_MODEL), 1.0),
        'x_sample': nrm(ks[1], (DEC_BATCH, DEC_SEQ, D_MODEL), 1.0),
        'state_conv': nrm(ks[2], (N_CONV_LAYERS, DEC_BATCH, CONV_WIDTH - 1, D_MODEL), 1.0),
        'cache_k': nrm(ks[3], (N_ATTN_LAYERS, n_pool, N_HEADS, PAGE_SIZE, HEAD_DIM), 1.0),
        'cache_v': nrm(ks[4], (N_ATTN_LAYERS, n_pool, N_HEADS, PAGE_SIZE, HEAD_DIM), 1.0),
        'page_table': perm.reshape(DEC_BATCH, n_pages).astype(jnp.int32),
        'norm_mix': 1.0 + nrm(ks[6], (DEPTH, D_MODEL), 0.02),
        'norm_ffn': 1.0 + nrm(ks[7], (DEPTH, D_MODEL), 0.02),
        'norm_final': 1.0 + nrm(ks[8], (D_MODEL,), 0.02),
        'w_conv_in': nrm(ks[9], (N_CONV_LAYERS, D_MODEL, 3 * D_MODEL), D_MODEL ** -0.5),
        'w_conv': nrm(ks[10], (N_CONV_LAYERS, CONV_WIDTH, D_MODEL), CONV_WIDTH ** -0.5),
        'w_conv_out': nrm(ks[11], (N_CONV_LAYERS, D_MODEL, D_MODEL), D_MODEL ** -0.5),
        'w_qkv': nrm(ks[12], (N_ATTN_LAYERS, D_MODEL, 3 * N_HEADS * HEAD_DIM), D_MODEL ** -0.5),
        'w_o': nrm(ks[13], (N_ATTN_LAYERS, N_HEADS * HEAD_DIM, D_MODEL), (N_HEADS * HEAD_DIM) ** -0.5),
        'w_gate_up': nrm(ks[14], (DEPTH, D_MODEL, 2 * D_FF), D_MODEL ** -0.5),
        'w_down': nrm(ks[15], (DEPTH, D_FF, D_MODEL), D_FF ** -0.5),
    }


def reference(x_prompt, x_sample, state_conv, cache_k, cache_v, page_table,
              norm_mix, norm_ffn, norm_final, w_conv_in, w_conv, w_conv_out,
              w_qkv, w_o, w_gate_up, w_down):
    B, S = x_prompt.shape[0], x_prompt.shape[1]
    T = x_sample.shape[1]
    pos_p = jnp.arange(S)
    pos_s = PAST_LEN + jnp.arange(T)
    yp, ys = x_prompt, x_sample
    conv_p, conv_s, kp_l, vp_l, ks_l, vs_l = [], [], [], [], [], []
    for i in range(DEPTH):
        hp = rms_norm(yp, norm_mix[i])
        hs = rms_norm(ys, norm_mix[i])
        if i % N_MIXERS == 0:
            c = i // N_MIXERS
            zero_state = jnp.zeros((B, CONV_WIDTH - 1, D_MODEL), hp.dtype)
            op, stp = conv_mixer(hp, zero_state, w_conv_in[c], w_conv[c], w_conv_out[c])
            os_, sts = conv_mixer(hs, state_conv[c], w_conv_in[c], w_conv[c], w_conv_out[c])
            conv_p.append(stp)
            conv_s.append(sts)
        else:
            a = i // N_MIXERS
            qp, kp, vp = qkv_rope(hp, w_qkv[a], pos_p)
            op = moba_prompt(qp, kp, vp) @ w_o[a]
            qs, kn, vn = qkv_rope(hs, w_qkv[a], pos_s)
            os_ = moba_sample(qs, kn, vn, cache_k, cache_v, a, page_table) @ w_o[a]
            kp_l.append(kp.reshape(B, S // PAGE_SIZE, PAGE_SIZE, N_HEADS, HEAD_DIM).transpose(0, 1, 3, 2, 4))
            vp_l.append(vp.reshape(B, S // PAGE_SIZE, PAGE_SIZE, N_HEADS, HEAD_DIM).transpose(0, 1, 3, 2, 4))
            ks_l.append(kn.transpose(0, 2, 1, 3))
            vs_l.append(vn.transpose(0, 2, 1, 3))
        yp = yp + op
        ys = ys + os_
        yp = yp + swiglu(rms_norm(yp, norm_ffn[i]), w_gate_up[i], w_down[i])
        ys = ys + swiglu(rms_norm(ys, norm_ffn[i]), w_gate_up[i], w_down[i])
    y_prompt = rms_norm(yp, norm_final)
    y_sample = rms_norm(ys, norm_final)
    return (y_prompt, y_sample, jnp.stack(conv_p), jnp.stack(conv_s),
            jnp.stack(kp_l), jnp.stack(vp_l), jnp.stack(ks_l), jnp.stack(vs_l))
```

```python
import functools

import jax
import jax.numpy as jnp
from jax import lax
from jax.experimental import pallas as pl
from jax.experimental.pallas import tpu as pltpu

D_MODEL = 1024
N_HEADS = 8
HEAD_DIM = D_MODEL // N_HEADS
ROT_DIM = HEAD_DIM // 4
ROPE_THETA = 500000.0
MOBA_BLOCK = 256
MOBA_TOPK = 3
PAGE_SIZE = 128
PAST_LEN = 16384
RMS_EPS = 1e-6
ATTN_SCALE = HEAD_DIM ** -0.5

VMEM_LIMIT_BYTES = 56 * 1024 * 1024
CONV_PAD = 8
KMEAN_PAGES_PER_STEP = 8

F32 = jnp.float32
BF16 = jnp.bfloat16
NEG_INF = float("-inf")


def _rms(x, g):
    return x * lax.rsqrt(jnp.mean(x * x, axis=-1, keepdims=True) + RMS_EPS) * g


def _rope(x, cos, sin_lo, sin_hi):
    half = ROT_DIM // 2
    return (x * cos
            + pltpu.roll(x, HEAD_DIM - half, axis=1) * sin_lo
            + pltpu.roll(x, half, axis=1) * sin_hi)


def _rope_tables(pos):
    half = ROT_DIM // 2
    inv = jnp.power(F32(ROPE_THETA), -jnp.arange(half, dtype=F32) * 2.0 / ROT_DIM)
    ang = pos.astype(F32)[:, None] * inv[None, :]
    cos, sin = jnp.cos(ang), jnp.sin(ang)
    t = pos.shape[0]
    rest = HEAD_DIM - ROT_DIM
    cos_f = jnp.concatenate([cos, cos, jnp.ones((t, rest), F32)], axis=1)
    sin_lo = jnp.concatenate([-sin, jnp.zeros((t, half + rest), F32)], axis=1)
    sin_hi = jnp.concatenate([jnp.zeros((t, half), F32), sin, jnp.zeros((t, rest), F32)], axis=1)
    return cos_f, sin_lo, sin_hi


def _params(*semantics):
    return pltpu.CompilerParams(dimension_semantics=semantics,
                                vmem_limit_bytes=VMEM_LIMIT_BYTES)


def _full(shape):
    return pl.BlockSpec(shape, lambda *_: (0,) * len(shape))


def _mix_prompt_kernel(x_ref, g_ref, win_ref, wc_ref, a_ref, st_ref, uext_ref):
    s = pl.program_id(1)
    tm = x_ref.shape[1]

    @pl.when(s == 0)
    def _():
        uext_ref[0:CONV_PAD, :] = jnp.zeros((CONV_PAD, D_MODEL), F32)

    h = _rms(x_ref[0], g_ref[...]).astype(BF16)
    bcx = jnp.dot(h, win_ref[...], preferred_element_type=F32)
    b = bcx[:, :D_MODEL]
    u = bcx[:, D_MODEL:2 * D_MODEL] * bcx[:, 2 * D_MODEL:]
    uext_ref[CONV_PAD:CONV_PAD + tm, :] = u
    um1 = uext_ref[CONV_PAD - 1:CONV_PAD - 1 + tm, :]
    um2 = uext_ref[CONV_PAD - 2:CONV_PAD - 2 + tm, :]
    conv = wc_ref[0:1, :] * um2 + wc_ref[1:2, :] * um1 + wc_ref[2:3, :] * u
    a_ref[0] = (b * conv).astype(BF16)
    tail = u[tm - 2:tm, :]
    uext_ref[CONV_PAD - 2:CONV_PAD, :] = tail

    @pl.when(s == pl.num_programs(1) - 1)
    def _():
        st_ref[0] = tail


def _mix_prompt(x, g, w_in, w_conv, tm):
    bsz, seq, _ = x.shape
    return pl.pallas_call(
        _mix_prompt_kernel,
        grid=(bsz, seq // tm),
        in_specs=[
            pl.BlockSpec((1, tm, D_MODEL), lambda b, s: (b, s, 0)),
            _full((1, D_MODEL)),
            _full((D_MODEL, 3 * D_MODEL)),
            _full((3, D_MODEL)),
        ],
        out_specs=[
            pl.BlockSpec((1, tm, D_MODEL), lambda b, s: (b, s, 0)),
            pl.BlockSpec((1, 2, D_MODEL), lambda b, s: (b, 0, 0)),
        ],
        out_shape=[
            jax.ShapeDtypeStruct((bsz, seq, D_MODEL), BF16),
            jax.ShapeDtypeStruct((bsz, 2, D_MODEL), F32),
        ],
        scratch_shapes=[pltpu.VMEM((CONV_PAD + tm, D_MODEL), F32)],
        compiler_params=_params("arbitrary", "arbitrary"),
        name="mix_prompt",
    )(x, g, w_in, w_conv)


def _mix_sample_kernel(x_ref, st_ref, g_ref, win_ref, wc_ref, a_ref, nst_ref):
    h = _rms(x_ref[...], g_ref[...]).astype(BF16)
    bcx = jnp.dot(h, win_ref[...], preferred_element_type=F32)
    b = bcx[:, :D_MODEL]
    u = bcx[:, D_MODEL:2 * D_MODEL] * bcx[:, 2 * D_MODEL:]
    um2 = st_ref[0]
    um1 = st_ref[1]
    conv = wc_ref[0:1, :] * um2 + wc_ref[1:2, :] * um1 + wc_ref[2:3, :] * u
    a_ref[...] = (b * conv).astype(BF16)
    nst_ref[0] = um1
    nst_ref[1] = u


def _mix_sample(x, state_t, g, w_in, w_conv):
    n = x.shape[0]
    return pl.pallas_call(
        _mix_sample_kernel,
        grid=(1,),
        in_specs=[
            _full((n, D_MODEL)),
            _full((2, n, D_MODEL)),
            _full((1, D_MODEL)),
            _full((D_MODEL, 3 * D_MODEL)),
            _full((3, D_MODEL)),
        ],
        out_specs=[_full((n, D_MODEL)), _full((2, n, D_MODEL))],
        out_shape=[
            jax.ShapeDtypeStruct((n, D_MODEL), BF16),
            jax.ShapeDtypeStruct((2, n, D_MODEL), F32),
        ],
        compiler_params=_params("arbitrary"),
        name="mix_sample",
    )(x, state_t, g, w_in, w_conv)


def _post_kernel(res_ref, a_ref, wpre_ref, g_ref, wgu_ref, wd_ref, gf_ref, y_ref,
                 *, final_norm):
    d_ff = wd_ref.shape[0]
    y1 = res_ref[...] + jnp.dot(a_ref[...], wpre_ref[...], preferred_element_type=F32)
    h = _rms(y1, g_ref[...]).astype(BF16)
    gu = jnp.dot(h, wgu_ref[...], preferred_element_type=F32)
    gate = gu[:, :d_ff]
    up = gu[:, d_ff:]
    act = (gate * (1.0 / (1.0 + jnp.exp(-gate))) * up).astype(BF16)
    y2 = y1 + jnp.dot(act, wd_ref[...], preferred_element_type=F32)
    if final_norm:
        y2 = _rms(y2, gf_ref[...])
    y_ref[...] = y2


def _post(res, a, w_pre, g_ffn, w_gu, w_down, g_final, tm, final_norm):
    m = res.shape[0]
    d_ff = w_down.shape[0]
    return pl.pallas_call(
        functools.partial(_post_kernel, final_norm=final_norm),
        grid=(m // tm,),
        in_specs=[
            pl.BlockSpec((tm, D_MODEL), lambda i: (i, 0)),
            pl.BlockSpec((tm, D_MODEL), lambda i: (i, 0)),
            _full((D_MODEL, D_MODEL)),
            _full((1, D_MODEL)),
            _full((D_MODEL, 2 * d_ff)),
            _full((d_ff, D_MODEL)),
            _full((1, D_MODEL)),
        ],
        out_specs=pl.BlockSpec((tm, D_MODEL), lambda i: (i, 0)),
        out_shape=jax.ShapeDtypeStruct((m, D_MODEL), F32),
        compiler_params=_params("arbitrary"),
        name="post_final" if final_norm else "post",
    )(res, a, w_pre, g_ffn, w_gu, w_down, g_final)


def _qkv_prompt_kernel(x_ref, g_ref, w_ref, cos_ref, slo_ref, shi_ref,
                       kp_ref, vp_ref, qt_ref, kb_ref, vt_ref, km_ref):
    tm = x_ref.shape[1]
    h = _rms(x_ref[0], g_ref[...]).astype(BF16)
    qkv = jnp.dot(h, w_ref[...], preferred_element_type=F32)
    cos, slo, shi = cos_ref[...], slo_ref[...], shi_ref[...]
    for hd in range(N_HEADS):
        lo = hd * HEAD_DIM
        q = _rope(qkv[:, lo:lo + HEAD_DIM], cos, slo, shi)
        k = _rope(qkv[:, D_MODEL + lo:D_MODEL + lo + HEAD_DIM], cos, slo, shi)
        v = qkv[:, 2 * D_MODEL + lo:2 * D_MODEL + lo + HEAD_DIM]
        for p in range(tm // PAGE_SIZE):
            kp_ref[0, p, hd] = k[p * PAGE_SIZE:(p + 1) * PAGE_SIZE]
            vp_ref[0, p, hd] = v[p * PAGE_SIZE:(p + 1) * PAGE_SIZE]
        for j in range(tm // MOBA_BLOCK):
            rows = slice(j * MOBA_BLOCK, (j + 1) * MOBA_BLOCK)
            qt_ref[0, hd, j] = q[rows].T
            kb_ref[0, hd, j] = k[rows].astype(BF16)
            vt_ref[0, hd, j] = v[rows].T.astype(BF16)
            km_ref[0, j, :, lo:lo + HEAD_DIM] = jnp.mean(k[rows], axis=0, keepdims=True)


def _qkv_prompt(x, g, w_qkv, cos, slo, shi, tm):
    bsz, seq, _ = x.shape
    n_pages, n_blk = seq // PAGE_SIZE, seq // MOBA_BLOCK
    tp, tb = tm // PAGE_SIZE, tm // MOBA_BLOCK
    tab = pl.BlockSpec((tm, HEAD_DIM), lambda b, s: (s, 0))
    return pl.pallas_call(
        _qkv_prompt_kernel,
        grid=(bsz, seq // tm),
        in_specs=[
            pl.BlockSpec((1, tm, D_MODEL), lambda b, s: (b, s, 0)),
            _full((1, D_MODEL)),
            _full((D_MODEL, 3 * D_MODEL)),
            tab, tab, tab,
        ],
        out_specs=[
            pl.BlockSpec((1, tp, N_HEADS, PAGE_SIZE, HEAD_DIM), lambda b, s: (b, s, 0, 0, 0)),
            pl.BlockSpec((1, tp, N_HEADS, PAGE_SIZE, HEAD_DIM), lambda b, s: (b, s, 0, 0, 0)),
            pl.BlockSpec((1, N_HEADS, tb, HEAD_DIM, MOBA_BLOCK), lambda b, s: (b, 0, s, 0, 0)),
            pl.BlockSpec((1, N_HEADS, tb, MOBA_BLOCK, HEAD_DIM), lambda b, s: (b, 0, s, 0, 0)),
            pl.BlockSpec((1, N_HEADS, tb, HEAD_DIM, MOBA_BLOCK), lambda b, s: (b, 0, s, 0, 0)),
            pl.BlockSpec((1, tb, 1, D_MODEL), lambda b, s: (b, s, 0, 0)),
        ],
        out_shape=[
            jax.ShapeDtypeStruct((bsz, n_pages, N_HEADS, PAGE_SIZE, HEAD_DIM), F32),
            jax.ShapeDtypeStruct((bsz, n_pages, N_HEADS, PAGE_SIZE, HEAD_DIM), F32),
            jax.ShapeDtypeStruct((bsz, N_HEADS, n_blk, HEAD_DIM, MOBA_BLOCK), F32),
            jax.ShapeDtypeStruct((bsz, N_HEADS, n_blk, MOBA_BLOCK, HEAD_DIM), BF16),
            jax.ShapeDtypeStruct((bsz, N_HEADS, n_blk, HEAD_DIM, MOBA_BLOCK), BF16),
            jax.ShapeDtypeStruct((bsz, n_blk, 1, D_MODEL), F32),
        ],
        compiler_params=_params("arbitrary", "arbitrary"),
        name="qkv_prompt",
    )(x, g, w_qkv, cos, slo, shi)


def _qkv_sample_kernel(x_ref, g_ref, w_ref, cos_ref, slo_ref, shi_ref, q_ref, k_ref, v_ref):
    h = _rms(x_ref[...], g_ref[...]).astype(BF16)
    qkv = jnp.dot(h, w_ref[...], preferred_element_type=F32)
    cos, slo, shi = cos_ref[...], slo_ref[...], shi_ref[...]
    for hd in range(N_HEADS):
        lo = hd * HEAD_DIM
        q_ref[:, lo:lo + HEAD_DIM] = _rope(qkv[:, lo:lo + HEAD_DIM], cos, slo, shi)
        k_ref[:, lo:lo + HEAD_DIM] = _rope(
            qkv[:, D_MODEL + lo:D_MODEL + lo + HEAD_DIM], cos, slo, shi)
    v_ref[...] = qkv[:, 2 * D_MODEL:]


def _qkv_sample(x, g, w_qkv, cos, slo, shi):
    n = x.shape[0]
    out = jax.ShapeDtypeStruct((n, D_MODEL), F32)
    return pl.pallas_call(
        _qkv_sample_kernel,
        grid=(1,),
        in_specs=[
            _full((n, D_MODEL)),
            _full((1, D_MODEL)),
            _full((D_MODEL, 3 * D_MODEL)),
            _full((n, HEAD_DIM)), _full((n, HEAD_DIM)), _full((n, HEAD_DIM)),
        ],
        out_specs=[_full((n, D_MODEL))] * 3,
        out_shape=[out, out, out],
        compiler_params=_params("arbitrary"),
        name="qkv_sample",
    )(x, g, w_qkv, cos, slo, shi)


def _attn_prompt_kernel(qt_ref, k_ref, vt_ref, km_ref, o_ref, bias_ref):
    qb = pl.program_id(2)
    n_blk = km_ref.shape[2]
    tq = qt_ref.shape[4]
    qt = qt_ref[0, 0, 0]
    gate = lax.dot_general(km_ref[0, 0], qt, (((1,), (0,)), ((), ())),
                           precision=lax.Precision.HIGHEST,
                           preferred_element_type=F32)
    blk = lax.broadcasted_iota(jnp.int32, (n_blk, tq), 0)
    past = blk < qb
    gate = jnp.where(past, gate, NEG_INF)
    rank = jnp.zeros((n_blk, tq), F32)
    for jp in range(n_blk):
        row = gate[jp:jp + 1, :]
        tie = jnp.where(blk > jp, 1.0, 0.0)
        rank = rank + jnp.where(row > gate, 1.0, jnp.where(row == gate, tie, 0.0))
    bias_ref[...] = jnp.where(rank < MOBA_TOPK, jnp.where(past, 0.0, NEG_INF), NEG_INF)

    qtb = qt.astype(BF16)
    s = jnp.dot(k_ref[0, 0, qb], qtb, preferred_element_type=F32) * ATTN_SCALE
    key = lax.broadcasted_iota(jnp.int32, s.shape, 0)
    qry = lax.broadcasted_iota(jnp.int32, s.shape, 1)
    s = jnp.where(key <= qry, s, NEG_INF)
    m0 = jnp.max(s, axis=0, keepdims=True)
    p = jnp.exp(s - m0)
    l0 = jnp.sum(p, axis=0, keepdims=True)
    acc0 = jnp.dot(vt_ref[0, 0, qb], p.astype(BF16), preferred_element_type=F32)

    def body(j, carry):
        m, l, acc = carry
        sj = jnp.dot(k_ref[0, 0, j], qtb, preferred_element_type=F32) * ATTN_SCALE
        sj = sj + bias_ref[pl.ds(j, 1), :]
        m_new = jnp.maximum(m, jnp.max(sj, axis=0, keepdims=True))
        alpha = jnp.exp(m - m_new)
        pj = jnp.exp(sj - m_new)
        l = alpha * l + jnp.sum(pj, axis=0, keepdims=True)
        acc = alpha * acc + jnp.dot(vt_ref[0, 0, j], pj.astype(BF16),
                                    preferred_element_type=F32)
        return m_new, l, acc

    _, l, acc = lax.fori_loop(0, qb, body, (m0, l0, acc0))
    o_ref[0] = (acc * (1.0 / l)).T.astype(BF16)


def _attn_prompt(qt, kb, vt, km):
    bsz, _, n_blk, _, tq = qt.shape
    return pl.pallas_call(
        _attn_prompt_kernel,
        grid=(bsz, N_HEADS, n_blk),
        in_specs=[
            pl.BlockSpec((1, 1, 1, HEAD_DIM, tq), lambda b, h, i: (b, h, i, 0, 0)),
            pl.BlockSpec((1, 1, n_blk, tq, HEAD_DIM), lambda b, h, i: (b, h, 0, 0, 0)),
            pl.BlockSpec((1, 1, n_blk, HEAD_DIM, tq), lambda b, h, i: (b, h, 0, 0, 0)),
            pl.BlockSpec((1, 1, n_blk, HEAD_DIM), lambda b, h, i: (b, h, 0, 0)),
        ],
        out_specs=pl.BlockSpec((1, tq, HEAD_DIM), lambda b, h, i: (b, i, h)),
        out_shape=jax.ShapeDtypeStruct((bsz, n_blk * tq, D_MODEL), BF16),
        scratch_shapes=[pltpu.VMEM((n_blk, tq), F32)],
        compiler_params=_params("arbitrary", "arbitrary", "arbitrary"),
        name="attn_prompt",
    )(qt, kb, vt, km)


def _kmean_sample_kernel(pt_ref, *refs):
    del pt_ref
    page_refs, out_ref = refs[:-1], refs[-1]
    ppb = MOBA_BLOCK // PAGE_SIZE
    for j in range(len(page_refs) // ppb):
        tot = jnp.sum(page_refs[j * ppb][0, 0], axis=1)
        for r in range(1, ppb):
            tot = tot + jnp.sum(page_refs[j * ppb + r][0, 0], axis=1)
        out_ref[0, j] = tot / MOBA_BLOCK


def _kmean_sample(cache_k, page_table, layer):
    n_seq, n_pages = page_table.shape
    ppb = MOBA_BLOCK // PAGE_SIZE
    pps = KMEAN_PAGES_PER_STEP
    page_block = (1, 1, N_HEADS, PAGE_SIZE, HEAD_DIM)

    def page_spec(r):
        return pl.BlockSpec(page_block, lambda b, i, pt: (layer, pt[b, i * pps + r], 0, 0, 0))

    return pl.pallas_call(
        _kmean_sample_kernel,
        grid_spec=pltpu.PrefetchScalarGridSpec(
            num_scalar_prefetch=1,
            grid=(n_seq, n_pages // pps),
            in_specs=[page_spec(r) for r in range(pps)],
            out_specs=pl.BlockSpec((1, pps // ppb, N_HEADS, HEAD_DIM),
                                   lambda b, i, pt: (b, i, 0, 0)),
        ),
        out_shape=jax.ShapeDtypeStruct((n_seq, n_pages // ppb, N_HEADS, HEAD_DIM), F32),
        compiler_params=_params("arbitrary", "arbitrary"),
        name="kmean_sample",
    )(page_table, *([cache_k] * pps))


def _select_sample_kernel(q_ref, km_ref, sel_ref):
    n_blk = km_ref.shape[1]
    prod = km_ref[0] * q_ref[0]
    gate = jnp.concatenate(
        [jnp.sum(prod[:, h * HEAD_DIM:(h + 1) * HEAD_DIM], axis=1, keepdims=True)
         for h in range(N_HEADS)], axis=1)
    blk = lax.broadcasted_iota(jnp.int32, gate.shape, 0)
    picks = []
    for _ in range(MOBA_TOPK):
        best = jnp.max(gate, axis=0, keepdims=True)
        idx = jnp.min(jnp.where(gate == best, blk, n_blk), axis=0, keepdims=True)
        picks.append(idx)
        gate = jnp.where(blk == idx, NEG_INF, gate)
    sel_ref[0] = jnp.concatenate(picks, axis=0)


def _select_sample(q, kmean):
    n_seq, n_blk = kmean.shape[0], kmean.shape[1]
    return pl.pallas_call(
        _select_sample_kernel,
        grid=(n_seq,),
        in_specs=[
            pl.BlockSpec((1, 1, D_MODEL), lambda b: (b, 0, 0)),
            pl.BlockSpec((1, n_blk, D_MODEL), lambda b: (b, 0, 0)),
        ],
        out_specs=pl.BlockSpec((1, MOBA_TOPK, N_HEADS), lambda b: (b, 0, 0)),
        out_shape=jax.ShapeDtypeStruct((n_seq, MOBA_TOPK, N_HEADS), jnp.int32),
        compiler_params=_params("arbitrary"),
        name="select_sample",
    )(q.reshape(n_seq, 1, D_MODEL), kmean.reshape(n_seq, n_blk, D_MODEL))


def _attn_sample_kernel(pt_ref, sel_ref, q_ref, kn_ref, vn_ref, *refs):
    del pt_ref, sel_ref
    n_sel_pages = (len(refs) - 1) // 2
    k_refs, v_refs, o_ref = refs[:n_sel_pages], refs[n_sel_pages:-1], refs[-1]
    q = q_ref[0, 0]
    s_new = jnp.sum(q * kn_ref[0, 0], axis=1, keepdims=True) * ATTN_SCALE
    scores = [jnp.sum(k_ref[0, 0, 0] * q, axis=1, keepdims=True) * ATTN_SCALE
              for k_ref in k_refs]
    m = s_new
    for s in scores:
        m = jnp.maximum(m, jnp.max(s, axis=0, keepdims=True))
    p_new = jnp.exp(s_new - m)
    l = p_new
    acc = p_new * vn_ref[0, 0]
    for s, v_ref in zip(scores, v_refs):
        p = jnp.exp(s - m)
        l = l + jnp.sum(p, axis=0, keepdims=True)
        acc = acc + jnp.sum(p * v_ref[0, 0, 0], axis=0, keepdims=True)
    o_ref[0, 0] = (acc * (1.0 / l)).astype(o_ref.dtype)


def _attn_sample(q, kn, vn, cache_k, cache_v, page_table, sel, layer):
    n_seq = q.shape[0]
    ppb = MOBA_BLOCK // PAGE_SIZE
    page_block = (1, 1, 1, PAGE_SIZE, HEAD_DIM)

    def page_spec(n, r):
        def index_map(b, h, pt, sl):
            blk = sl[(b * MOBA_TOPK + n) * N_HEADS + h]
            return (layer, pt[b, blk * ppb + r], h, 0, 0)
        return pl.BlockSpec(page_block, index_map)

    row = pl.BlockSpec((1, 1, 1, HEAD_DIM), lambda b, h, pt, sl: (b, h, 0, 0))
    pages = [page_spec(n, r) for n in range(MOBA_TOPK) for r in range(ppb)]
    return pl.pallas_call(
        _attn_sample_kernel,
        grid_spec=pltpu.PrefetchScalarGridSpec(
            num_scalar_prefetch=2,
            grid=(n_seq, N_HEADS),
            in_specs=[row, row, row] + pages + pages,
            out_specs=row,
        ),
        out_shape=jax.ShapeDtypeStruct((n_seq, N_HEADS, 1, HEAD_DIM), BF16),
        compiler_params=_params("arbitrary", "arbitrary"),
        name="attn_sample",
    )(page_table, sel, q, kn, vn, *([cache_k] * len(pages)), *([cache_v] * len(pages)))


def kernel(x_prompt, x_sample, state_conv, cache_k, cache_v, page_table,
           norm_mix, norm_ffn, norm_final, w_conv_in, w_conv, w_conv_out,
           w_qkv, w_o, w_gate_up, w_down):
    bsz, seq, _ = x_prompt.shape
    n_seq = x_sample.shape[0]
    tm_prompt = 256
    row = lambda v: v.reshape(1, D_MODEL)
    w_in_b = w_conv_in[0].astype(BF16)
    w_out_b = w_conv_out[0].astype(BF16)
    w_qkv_b = w_qkv[0].astype(BF16)
    w_o_b = w_o[0].astype(BF16)
    w_gu_b = w_gate_up.astype(BF16)
    w_down_b = w_down.astype(BF16)
    g_final = row(norm_final)

    a_p, conv_p = _mix_prompt(x_prompt, row(norm_mix[0]), w_in_b, w_conv[0], tm_prompt)
    y_p = _post(x_prompt.reshape(bsz * seq, D_MODEL), a_p.reshape(bsz * seq, D_MODEL),
                w_out_b, row(norm_ffn[0]), w_gu_b[0], w_down_b[0], g_final,
                tm_prompt, False)

    xs = x_sample.reshape(n_seq, D_MODEL)
    a_s, conv_s = _mix_sample(xs, state_conv[0].transpose(1, 0, 2), row(norm_mix[0]),
                              w_in_b, w_conv[0])
    y_s = _post(xs, a_s, w_out_b, row(norm_ffn[0]), w_gu_b[0], w_down_b[0], g_final,
                n_seq, False)

    cos, slo, shi = _rope_tables(jnp.arange(seq))
    k_pages, v_pages, qt, kb, vt, km = _qkv_prompt(
        y_p.reshape(bsz, seq, D_MODEL), row(norm_mix[1]), w_qkv_b, cos, slo, shi, tm_prompt)
    n_blk = seq // MOBA_BLOCK
    km_h = km.reshape(bsz, n_blk, N_HEADS, HEAD_DIM).transpose(0, 2, 1, 3)
    attn_p = _attn_prompt(qt, kb, vt, km_h)
    y_p = _post(y_p, attn_p.reshape(bsz * seq, D_MODEL), w_o_b, row(norm_ffn[1]),
                w_gu_b[1], w_down_b[1], g_final, tm_prompt, True)

    cos_s, slo_s, shi_s = _rope_tables(PAST_LEN + jnp.arange(1))
    bc = lambda t: jnp.broadcast_to(t, (n_seq, HEAD_DIM))
    q_s, k_s, v_s = _qkv_sample(y_s, row(norm_mix[1]), w_qkv_b, bc(cos_s), bc(slo_s), bc(shi_s))
    kmean_s = _kmean_sample(cache_k, page_table, 0)
    sel = _select_sample(q_s, kmean_s)
    heads = lambda t: t.reshape(n_seq, N_HEADS, 1, HEAD_DIM)
    attn_s = _attn_sample(heads(q_s), heads(k_s), heads(v_s), cache_k, cache_v,
                          page_table, sel.reshape(-1), 0)
    y_s = _post(y_s, attn_s.reshape(n_seq, D_MODEL), w_o_b, row(norm_ffn[1]),
                w_gu_b[1], w_down_b[1], g_final, n_seq, True)

    return (y_p.reshape(bsz, seq, D_MODEL),
            y_s.reshape(n_seq, 1, D_MODEL),
            conv_p[None],
            conv_s.transpose(1, 0, 2)[None],
            k_pages[None],
            v_pages[None],
            heads(k_s)[None],
            heads(v_s)[None])
```

```python
import functools

import jax
import jax.numpy as jnp
from jax import lax
from jax.experimental import pallas as pl
from jax.experimental.pallas import tpu as pltpu

D_MODEL = 1024
N_HEADS = 8
HEAD_DIM = D_MODEL // N_HEADS
ROT_DIM = HEAD_DIM // 4
ROPE_THETA = 500000.0
MOBA_BLOCK = 256
MOBA_TOPK = 3
PAGE_SIZE = 128
PAST_LEN = 16384
RMS_EPS = 1e-6
ATTN_SCALE = HEAD_DIM ** -0.5
LOG2_E = 1.4426950408889634

VMEM_LIMIT_BYTES = 56 * 1024 * 1024
CONV_PAD = 8
VT_ONES_ROWS = 16
ATTN_HEADS_PER_STEP = 4
KMEAN_PAGES_PER_STEP = 8

F32 = jnp.float32
BF16 = jnp.bfloat16
NEG_INF = float("-inf")


def _rms(x, g):
    return x * lax.rsqrt(jnp.mean(x * x, axis=-1, keepdims=True) + RMS_EPS) * g


def _rope(x, cos, sin_lo, sin_hi):
    half = ROT_DIM // 2
    return (x * cos
            + pltpu.roll(x, HEAD_DIM - half, axis=1) * sin_lo
            + pltpu.roll(x, half, axis=1) * sin_hi)


def _rope_tables(pos):
    half = ROT_DIM // 2
    inv = jnp.power(F32(ROPE_THETA), -jnp.arange(half, dtype=F32) * 2.0 / ROT_DIM)
    ang = pos.astype(F32)[:, None] * inv[None, :]
    cos, sin = jnp.cos(ang), jnp.sin(ang)
    t = pos.shape[0]
    rest = HEAD_DIM - ROT_DIM
    cos_f = jnp.concatenate([cos, cos, jnp.ones((t, rest), F32)], axis=1)
    sin_lo = jnp.concatenate([-sin, jnp.zeros((t, half + rest), F32)], axis=1)
    sin_hi = jnp.concatenate([jnp.zeros((t, half), F32), sin, jnp.zeros((t, rest), F32)], axis=1)
    return cos_f, sin_lo, sin_hi


def _params(*semantics):
    return pltpu.CompilerParams(dimension_semantics=semantics,
                                vmem_limit_bytes=VMEM_LIMIT_BYTES)


def _full(shape):
    return pl.BlockSpec(shape, lambda *_: (0,) * len(shape))


def _mix_prompt_kernel(x_ref, g_ref, win_ref, wc_ref, a_ref, st_ref, uext_ref):
    s = pl.program_id(1)
    tm = x_ref.shape[1]

    @pl.when(s == 0)
    def _():
        uext_ref[0:CONV_PAD, :] = jnp.zeros((CONV_PAD, D_MODEL), F32)

    h = _rms(x_ref[0], g_ref[...]).astype(BF16)
    bcx = jnp.dot(h, win_ref[...], preferred_element_type=F32)
    b = bcx[:, :D_MODEL]
    u = bcx[:, D_MODEL:2 * D_MODEL] * bcx[:, 2 * D_MODEL:]
    uext_ref[CONV_PAD:CONV_PAD + tm, :] = u
    um1 = uext_ref[CONV_PAD - 1:CONV_PAD - 1 + tm, :]
    um2 = uext_ref[CONV_PAD - 2:CONV_PAD - 2 + tm, :]
    conv = wc_ref[0:1, :] * um2 + wc_ref[1:2, :] * um1 + wc_ref[2:3, :] * u
    a_ref[0] = (b * conv).astype(BF16)
    tail = u[tm - 2:tm, :]
    uext_ref[CONV_PAD - 2:CONV_PAD, :] = tail

    @pl.when(s == pl.num_programs(1) - 1)
    def _():
        st_ref[0] = tail


def _mix_prompt(x, g, w_in, w_conv, tm):
    bsz, seq, _ = x.shape
    return pl.pallas_call(
        _mix_prompt_kernel,
        grid=(bsz, seq // tm),
        in_specs=[
            pl.BlockSpec((1, tm, D_MODEL), lambda b, s: (b, s, 0)),
            _full((1, D_MODEL)),
            _full((D_MODEL, 3 * D_MODEL)),
            _full((3, D_MODEL)),
        ],
        out_specs=[
            pl.BlockSpec((1, tm, D_MODEL), lambda b, s: (b, s, 0)),
            pl.BlockSpec((1, 2, D_MODEL), lambda b, s: (b, 0, 0)),
        ],
        out_shape=[
            jax.ShapeDtypeStruct((bsz, seq, D_MODEL), BF16),
            jax.ShapeDtypeStruct((bsz, 2, D_MODEL), F32),
        ],
        scratch_shapes=[pltpu.VMEM((CONV_PAD + tm, D_MODEL), F32)],
        compiler_params=_params("arbitrary", "arbitrary"),
        name="mix_prompt",
    )(x, g, w_in, w_conv)


def _mix_sample_kernel(x_ref, st_ref, g_ref, win_ref, wc_ref, a_ref, nst_ref):
    h = _rms(x_ref[...], g_ref[...]).astype(BF16)
    bcx = jnp.dot(h, win_ref[...], preferred_element_type=F32)
    b = bcx[:, :D_MODEL]
    u = bcx[:, D_MODEL:2 * D_MODEL] * bcx[:, 2 * D_MODEL:]
    um2 = st_ref[0]
    um1 = st_ref[1]
    conv = wc_ref[0:1, :] * um2 + wc_ref[1:2, :] * um1 + wc_ref[2:3, :] * u
    a_ref[...] = (b * conv).astype(BF16)
    nst_ref[0] = um1
    nst_ref[1] = u


def _mix_sample(x, state_t, g, w_in, w_conv):
    n = x.shape[0]
    return pl.pallas_call(
        _mix_sample_kernel,
        grid=(1,),
        in_specs=[
            _full((n, D_MODEL)),
            _full((2, n, D_MODEL)),
            _full((1, D_MODEL)),
            _full((D_MODEL, 3 * D_MODEL)),
            _full((3, D_MODEL)),
        ],
        out_specs=[_full((n, D_MODEL)), _full((2, n, D_MODEL))],
        out_shape=[
            jax.ShapeDtypeStruct((n, D_MODEL), BF16),
            jax.ShapeDtypeStruct((2, n, D_MODEL), F32),
        ],
        compiler_params=_params("arbitrary"),
        name="mix_sample",
    )(x, state_t, g, w_in, w_conv)


def _post_kernel(res_ref, a_ref, wpre_ref, g_ref, wgu_ref, wd_ref, gf_ref, y_ref,
                 *, final_norm):
    d_ff = wd_ref.shape[0]
    y1 = res_ref[...] + jnp.dot(a_ref[...], wpre_ref[...], preferred_element_type=F32)
    h = _rms(y1, g_ref[...]).astype(BF16)
    gu = jnp.dot(h, wgu_ref[...], preferred_element_type=F32)
    gate = gu[:, :d_ff]
    up = gu[:, d_ff:]
    act = (gate * (1.0 / (1.0 + jnp.exp(-gate))) * up).astype(BF16)
    y2 = y1 + jnp.dot(act, wd_ref[...], preferred_element_type=F32)
    if final_norm:
        y2 = _rms(y2, gf_ref[...])
    y_ref[...] = y2


def _post(res, a, w_pre, g_ffn, w_gu, w_down, g_final, tm, final_norm):
    m = res.shape[0]
    d_ff = w_down.shape[0]
    return pl.pallas_call(
        functools.partial(_post_kernel, final_norm=final_norm),
        grid=(m // tm,),
        in_specs=[
            pl.BlockSpec((tm, D_MODEL), lambda i: (i, 0)),
            pl.BlockSpec((tm, D_MODEL), lambda i: (i, 0)),
            _full((D_MODEL, D_MODEL)),
            _full((1, D_MODEL)),
            _full((D_MODEL, 2 * d_ff)),
            _full((d_ff, D_MODEL)),
            _full((1, D_MODEL)),
        ],
        out_specs=pl.BlockSpec((tm, D_MODEL), lambda i: (i, 0)),
        out_shape=jax.ShapeDtypeStruct((m, D_MODEL), F32),
        compiler_params=_params("arbitrary"),
        name="post_final" if final_norm else "post",
    )(res, a, w_pre, g_ffn, w_gu, w_down, g_final)


def _qkv_prompt_kernel(x_ref, g_ref, w_ref, cos_ref, slo_ref, shi_ref,
                       kp_ref, vp_ref, qt_ref, kb_ref, vt_ref, km_ref):
    tm = x_ref.shape[1]
    h = _rms(x_ref[0], g_ref[...]).astype(BF16)
    qkv = jnp.dot(h, w_ref[...], preferred_element_type=F32)
    cos, slo, shi = cos_ref[...], slo_ref[...], shi_ref[...]
    for hd in range(N_HEADS):
        lo = hd * HEAD_DIM
        q = _rope(qkv[:, lo:lo + HEAD_DIM], cos, slo, shi)
        k = _rope(qkv[:, D_MODEL + lo:D_MODEL + lo + HEAD_DIM], cos, slo, shi)
        v = qkv[:, 2 * D_MODEL + lo:2 * D_MODEL + lo + HEAD_DIM]
        for p in range(tm // PAGE_SIZE):
            kp_ref[0, p, hd] = k[p * PAGE_SIZE:(p + 1) * PAGE_SIZE]
            vp_ref[0, p, hd] = v[p * PAGE_SIZE:(p + 1) * PAGE_SIZE]
        for j in range(tm // MOBA_BLOCK):
            rows = slice(j * MOBA_BLOCK, (j + 1) * MOBA_BLOCK)
            qt_ref[0, hd, j] = q[rows].T
            kb_ref[0, hd, j] = k[rows].astype(BF16)
            vt_ref[0, hd, j, :HEAD_DIM] = v[rows].T.astype(BF16)
            vt_ref[0, hd, j, HEAD_DIM:] = jnp.ones((VT_ONES_ROWS, MOBA_BLOCK), BF16)
            km_ref[0, j, :, lo:lo + HEAD_DIM] = jnp.mean(k[rows], axis=0, keepdims=True)


def _qkv_prompt(x, g, w_qkv, cos, slo, shi, tm):
    bsz, seq, _ = x.shape
    n_pages, n_blk = seq // PAGE_SIZE, seq // MOBA_BLOCK
    tp, tb = tm // PAGE_SIZE, tm // MOBA_BLOCK
    tab = pl.BlockSpec((tm, HEAD_DIM), lambda b, s: (s, 0))
    return pl.pallas_call(
        _qkv_prompt_kernel,
        grid=(bsz, seq // tm),
        in_specs=[
            pl.BlockSpec((1, tm, D_MODEL), lambda b, s: (b, s, 0)),
            _full((1, D_MODEL)),
            _full((D_MODEL, 3 * D_MODEL)),
            tab, tab, tab,
        ],
        out_specs=[
            pl.BlockSpec((1, tp, N_HEADS, PAGE_SIZE, HEAD_DIM), lambda b, s: (b, s, 0, 0, 0)),
            pl.BlockSpec((1, tp, N_HEADS, PAGE_SIZE, HEAD_DIM), lambda b, s: (b, s, 0, 0, 0)),
            pl.BlockSpec((1, N_HEADS, tb, HEAD_DIM, MOBA_BLOCK), lambda b, s: (b, 0, s, 0, 0)),
            pl.BlockSpec((1, N_HEADS, tb, MOBA_BLOCK, HEAD_DIM), lambda b, s: (b, 0, s, 0, 0)),
            pl.BlockSpec((1, N_HEADS, tb, HEAD_DIM + VT_ONES_ROWS, MOBA_BLOCK),
                         lambda b, s: (b, 0, s, 0, 0)),
            pl.BlockSpec((1, tb, 1, D_MODEL), lambda b, s: (b, s, 0, 0)),
        ],
        out_shape=[
            jax.ShapeDtypeStruct((bsz, n_pages, N_HEADS, PAGE_SIZE, HEAD_DIM), F32),
            jax.ShapeDtypeStruct((bsz, n_pages, N_HEADS, PAGE_SIZE, HEAD_DIM), F32),
            jax.ShapeDtypeStruct((bsz, N_HEADS, n_blk, HEAD_DIM, MOBA_BLOCK), F32),
            jax.ShapeDtypeStruct((bsz, N_HEADS, n_blk, MOBA_BLOCK, HEAD_DIM), BF16),
            jax.ShapeDtypeStruct((bsz, N_HEADS, n_blk, HEAD_DIM + VT_ONES_ROWS, MOBA_BLOCK), BF16),
            jax.ShapeDtypeStruct((bsz, n_blk, 1, D_MODEL), F32),
        ],
        compiler_params=_params("arbitrary", "arbitrary"),
        name="qkv_prompt",
    )(x, g, w_qkv, cos, slo, shi)


def _qkv_sample_kernel(x_ref, g_ref, w_ref, cos_ref, slo_ref, shi_ref, q_ref, k_ref, v_ref):
    h = _rms(x_ref[...], g_ref[...]).astype(BF16)
    qkv = jnp.dot(h, w_ref[...], preferred_element_type=F32)
    cos, slo, shi = cos_ref[...], slo_ref[...], shi_ref[...]
    for hd in range(N_HEADS):
        lo = hd * HEAD_DIM
        q_ref[:, lo:lo + HEAD_DIM] = _rope(qkv[:, lo:lo + HEAD_DIM], cos, slo, shi)
        k_ref[:, lo:lo + HEAD_DIM] = _rope(
            qkv[:, D_MODEL + lo:D_MODEL + lo + HEAD_DIM], cos, slo, shi)
    v_ref[...] = qkv[:, 2 * D_MODEL:]


def _qkv_sample(x, g, w_qkv, cos, slo, shi):
    n = x.shape[0]
    out = jax.ShapeDtypeStruct((n, D_MODEL), F32)
    return pl.pallas_call(
        _qkv_sample_kernel,
        grid=(1,),
        in_specs=[
            _full((n, D_MODEL)),
            _full((1, D_MODEL)),
            _full((D_MODEL, 3 * D_MODEL)),
            _full((n, HEAD_DIM)), _full((n, HEAD_DIM)), _full((n, HEAD_DIM)),
        ],
        out_specs=[_full((n, D_MODEL))] * 3,
        out_shape=[out, out, out],
        compiler_params=_params("arbitrary"),
        name="qkv_sample",
    )(x, g, w_qkv, cos, slo, shi)


def _attn_prompt_kernel(qt_ref, k_ref, vt_ref, km_ref, o_ref,
                        bias_ref, qs_ref, m_ref, mb_ref, alpha_ref, p_ref, s_ref, acc_ref):
    qb = pl.program_id(2)
    n_heads = qt_ref.shape[1]
    n_blk = km_ref.shape[2]
    tq = qt_ref.shape[4]
    heads = range(n_heads)
    blk = lax.broadcasted_iota(jnp.int32, (n_blk, tq), 0)
    past = blk < qb
    key = lax.broadcasted_iota(jnp.int32, (tq, tq), 0)
    qry = lax.broadcasted_iota(jnp.int32, (tq, tq), 1)
    causal = jnp.where(key <= qry, 0.0, NEG_INF)

    def score_stage(j, slot):
        scores = [jnp.dot(k_ref[0, h, j], qs_ref[h], preferred_element_type=F32)
                  for h in heads]
        for h in heads:
            s = scores[h] + bias_ref[h, pl.ds(j, 1), :]
            s_ref[slot, h] = s
            mb_ref[slot, h] = jnp.max(s, axis=0, keepdims=True)

    def value_stage(j, slot):
        pvs = [jnp.dot(vt_ref[0, h, j], p_ref[slot, h], preferred_element_type=F32)
               for h in heads]
        for h in heads:
            acc_ref[h] = alpha_ref[slot, h] * acc_ref[h] + pvs[h]

    gates = [lax.dot_general(km_ref[0, h], qt_ref[0, h, 0], (((1,), (0,)), ((), ())),
                             precision=lax.Precision.HIGHEST,
                             preferred_element_type=F32) for h in heads]
    for h in heads:
        qs_ref[h] = (qt_ref[0, h, 0] * (ATTN_SCALE * LOG2_E)).astype(BF16)
    own = [jnp.dot(k_ref[0, h, qb], qs_ref[h], preferred_element_type=F32) for h in heads]
    for h in heads:
        gate = jnp.where(past, gates[h], NEG_INF)
        rank = jnp.zeros((n_blk, tq), F32)
        for jp in range(n_blk):
            row = gate[jp:jp + 1, :]
            tie = jnp.where(blk > jp, 1.0, 0.0)
            rank = rank + jnp.where(row > gate, 1.0, jnp.where(row == gate, tie, 0.0))
        bias_ref[h] = jnp.where(rank < MOBA_TOPK, jnp.where(past, 0.0, NEG_INF), NEG_INF)
    for h in heads:
        s = own[h] + causal
        m = jnp.max(s, axis=0, keepdims=True)
        m_ref[h] = m
        alpha_ref[1, h] = jnp.ones((1, tq), F32)
        p_ref[1, h] = jnp.exp2(s - m).astype(BF16)
        acc_ref[h] = jnp.zeros(acc_ref.shape[1:], F32)
    score_stage(0, 0)

    def trip(j, cur):
        nxt = 1 - cur
        value_stage(jnp.where(j == 0, qb, j - 1), nxt)
        for h in heads:
            m_old = m_ref[h]
            m_new = jnp.maximum(m_old, mb_ref[cur, h])
            m_ref[h] = m_new
            alpha_ref[cur, h] = jnp.exp2(m_old - m_new)
            p_ref[cur, h] = jnp.exp2(s_ref[cur, h] - m_new).astype(BF16)
        score_stage(jnp.minimum(j + 1, n_blk - 1), nxt)

    def body(i, carry):
        trip(2 * i, 0)

        @pl.when(2 * i + 1 < qb)
        def _():
            trip(2 * i + 1, 1)

        return carry

    lax.fori_loop(0, (qb + 1) // 2, body, 0)
    value_stage(jnp.where(qb == 0, qb, qb - 1), lax.rem(qb + 1, 2))
    for h in heads:
        o = acc_ref[h, :HEAD_DIM] * (1.0 / acc_ref[h, HEAD_DIM:HEAD_DIM + 1])
        o_ref[0, :, h * HEAD_DIM:(h + 1) * HEAD_DIM] = o.T.astype(BF16)


def _attn_prompt(qt, kb, vt, km):
    bsz, _, n_blk, _, tq = qt.shape
    vt_rows = vt.shape[3]
    hps = ATTN_HEADS_PER_STEP
    return pl.pallas_call(
        _attn_prompt_kernel,
        grid=(bsz, N_HEADS // hps, n_blk),
        in_specs=[
            pl.BlockSpec((1, hps, 1, HEAD_DIM, tq), lambda b, h, i: (b, h, i, 0, 0)),
            pl.BlockSpec((1, hps, n_blk, tq, HEAD_DIM), lambda b, h, i: (b, h, 0, 0, 0)),
            pl.BlockSpec((1, hps, n_blk, vt_rows, tq), lambda b, h, i: (b, h, 0, 0, 0)),
            pl.BlockSpec((1, hps, n_blk, HEAD_DIM), lambda b, h, i: (b, h, 0, 0)),
        ],
        out_specs=pl.BlockSpec((1, tq, hps * HEAD_DIM), lambda b, h, i: (b, i, h)),
        out_shape=jax.ShapeDtypeStruct((bsz, n_blk * tq, D_MODEL), BF16),
        scratch_shapes=[
            pltpu.VMEM((hps, n_blk, tq), F32),
            pltpu.VMEM((hps, HEAD_DIM, tq), BF16),
            pltpu.VMEM((hps, 1, tq), F32),
            pltpu.VMEM((2, hps, 1, tq), F32),
            pltpu.VMEM((2, hps, 1, tq), F32),
            pltpu.VMEM((2, hps, tq, tq), BF16),
            pltpu.VMEM((2, hps, tq, tq), F32),
            pltpu.VMEM((hps, vt_rows, tq), F32),
        ],
        compiler_params=_params("arbitrary", "arbitrary", "arbitrary"),
        name="attn_prompt",
    )(qt, kb, vt, km)


def _kmean_sample_kernel(pt_ref, *refs):
    del pt_ref
    page_refs, out_ref = refs[:-1], refs[-1]
    ppb = MOBA_BLOCK // PAGE_SIZE
    for j in range(len(page_refs) // ppb):
        tot = jnp.sum(page_refs[j * ppb][0, 0], axis=1)
        for r in range(1, ppb):
            tot = tot + jnp.sum(page_refs[j * ppb + r][0, 0], axis=1)
        out_ref[0, j] = tot / MOBA_BLOCK


def _kmean_sample(cache_k, page_table, layer):
    n_seq, n_pages = page_table.shape
    ppb = MOBA_BLOCK // PAGE_SIZE
    pps = KMEAN_PAGES_PER_STEP
    page_block = (1, 1, N_HEADS, PAGE_SIZE, HEAD_DIM)

    def page_spec(r):
        return pl.BlockSpec(page_block, lambda b, i, pt: (layer, pt[b, i * pps + r], 0, 0, 0))

    return pl.pallas_call(
        _kmean_sample_kernel,
        grid_spec=pltpu.PrefetchScalarGridSpec(
            num_scalar_prefetch=1,
            grid=(n_seq, n_pages // pps),
            in_specs=[page_spec(r) for r in range(pps)],
            out_specs=pl.BlockSpec((1, pps // ppb, N_HEADS, HEAD_DIM),
                                   lambda b, i, pt: (b, i, 0, 0)),
        ),
        out_shape=jax.ShapeDtypeStruct((n_seq, n_pages // ppb, N_HEADS, HEAD_DIM), F32),
        compiler_params=_params("arbitrary", "arbitrary"),
        name="kmean_sample",
    )(page_table, *([cache_k] * pps))


def _select_sample_kernel(q_ref, km_ref, sel_ref):
    n_blk = km_ref.shape[1]
    prod = km_ref[0] * q_ref[0]
    gate = jnp.concatenate(
        [jnp.sum(prod[:, h * HEAD_DIM:(h + 1) * HEAD_DIM], axis=1, keepdims=True)
         for h in range(N_HEADS)], axis=1)
    blk = lax.broadcasted_iota(jnp.int32, gate.shape, 0)
    picks = []
    for _ in range(MOBA_TOPK):
        best = jnp.max(gate, axis=0, keepdims=True)
        idx = jnp.min(jnp.where(gate == best, blk, n_blk), axis=0, keepdims=True)
        picks.append(idx)
        gate = jnp.where(blk == idx, NEG_INF, gate)
    sel_ref[0] = jnp.concatenate(picks, axis=0)


def _select_sample(q, kmean):
    n_seq, n_blk = kmean.shape[0], kmean.shape[1]
    return pl.pallas_call(
        _select_sample_kernel,
        grid=(n_seq,),
        in_specs=[
            pl.BlockSpec((1, 1, D_MODEL), lambda b: (b, 0, 0)),
            pl.BlockSpec((1, n_blk, D_MODEL), lambda b: (b, 0, 0)),
        ],
        out_specs=pl.BlockSpec((1, MOBA_TOPK, N_HEADS), lambda b: (b, 0, 0)),
        out_shape=jax.ShapeDtypeStruct((n_seq, MOBA_TOPK, N_HEADS), jnp.int32),
        compiler_params=_params("arbitrary"),
        name="select_sample",
    )(q.reshape(n_seq, 1, D_MODEL), kmean.reshape(n_seq, n_blk, D_MODEL))


def _attn_sample_kernel(pt_ref, sel_ref, q_ref, kn_ref, vn_ref, *refs):
    del pt_ref, sel_ref
    n_sel_pages = (len(refs) - 1) // 2
    k_refs, v_refs, o_ref = refs[:n_sel_pages], refs[n_sel_pages:-1], refs[-1]
    q = q_ref[0, 0]
    s_new = jnp.sum(q * kn_ref[0, 0], axis=1, keepdims=True) * ATTN_SCALE
    scores = [jnp.sum(k_ref[0, 0, 0] * q, axis=1, keepdims=True) * ATTN_SCALE
              for k_ref in k_refs]
    m = s_new
    for s in scores:
        m = jnp.maximum(m, jnp.max(s, axis=0, keepdims=True))
    p_new = jnp.exp(s_new - m)
    l = p_new
    acc = p_new * vn_ref[0, 0]
    for s, v_ref in zip(scores, v_refs):
        p = jnp.exp(s - m)
        l = l + jnp.sum(p, axis=0, keepdims=True)
        acc = acc + jnp.sum(p * v_ref[0, 0, 0], axis=0, keepdims=True)
    o_ref[0, 0] = (acc * (1.0 / l)).astype(o_ref.dtype)


def _attn_sample(q, kn, vn, cache_k, cache_v, page_table, sel, layer):
    n_seq = q.shape[0]
    ppb = MOBA_BLOCK // PAGE_SIZE
    page_block = (1, 1, 1, PAGE_SIZE, HEAD_DIM)

    def page_spec(n, r):
        def index_map(b, h, pt, sl):
            blk = sl[(b * MOBA_TOPK + n) * N_HEADS + h]
            return (layer, pt[b, blk * ppb + r], h, 0, 0)
        return pl.BlockSpec(page_block, index_map)

    row = pl.BlockSpec((1, 1, 1, HEAD_DIM), lambda b, h, pt, sl: (b, h, 0, 0))
    pages = [page_spec(n, r) for n in range(MOBA_TOPK) for r in range(ppb)]
    return pl.pallas_call(
        _attn_sample_kernel,
        grid_spec=pltpu.PrefetchScalarGridSpec(
            num_scalar_prefetch=2,
            grid=(n_seq, N_HEADS),
            in_specs=[row, row, row] + pages + pages,
            out_specs=row,
        ),
        out_shape=jax.ShapeDtypeStruct((n_seq, N_HEADS, 1, HEAD_DIM), BF16),
        compiler_params=_params("arbitrary", "arbitrary"),
        name="attn_sample",
    )(page_table, sel, q, kn, vn, *([cache_k] * len(pages)), *([cache_v] * len(pages)))


def kernel(x_prompt, x_sample, state_conv, cache_k, cache_v, page_table,
           norm_mix, norm_ffn, norm_final, w_conv_in, w_conv, w_conv_out,
           w_qkv, w_o, w_gate_up, w_down):
    bsz, seq, _ = x_prompt.shape
    n_seq = x_sample.shape[0]
    tm_prompt = 512
    row = lambda v: v.reshape(1, D_MODEL)
    w_in_b = w_conv_in[0].astype(BF16)
    w_out_b = w_conv_out[0].astype(BF16)
    w_qkv_b = w_qkv[0].astype(BF16)
    w_o_b = w_o[0].astype(BF16)
    w_gu_b = w_gate_up.astype(BF16)
    w_down_b = w_down.astype(BF16)
    g_final = row(norm_final)

    a_p, conv_p = _mix_prompt(x_prompt, row(norm_mix[0]), w_in_b, w_conv[0], tm_prompt)
    y_p = _post(x_prompt.reshape(bsz * seq, D_MODEL), a_p.reshape(bsz * seq, D_MODEL),
                w_out_b, row(norm_ffn[0]), w_gu_b[0], w_down_b[0], g_final,
                tm_prompt, False)

    xs = x_sample.reshape(n_seq, D_MODEL)
    a_s, conv_s = _mix_sample(xs, state_conv[0].transpose(1, 0, 2), row(norm_mix[0]),
                              w_in_b, w_conv[0])
    y_s = _post(xs, a_s, w_out_b, row(norm_ffn[0]), w_gu_b[0], w_down_b[0], g_final,
                n_seq, False)

    cos, slo, shi = _rope_tables(jnp.arange(seq))
    k_pages, v_pages, qt, kb, vt, km = _qkv_prompt(
        y_p.reshape(bsz, seq, D_MODEL), row(norm_mix[1]), w_qkv_b, cos, slo, shi, tm_prompt)
    n_blk = seq // MOBA_BLOCK
    km_h = km.reshape(bsz, n_blk, N_HEADS, HEAD_DIM).transpose(0, 2, 1, 3)
    attn_p = _attn_prompt(qt, kb, vt, km_h)
    y_p = _post(y_p, attn_p.reshape(bsz * seq, D_MODEL), w_o_b, row(norm_ffn[1]),
                w_gu_b[1], w_down_b[1], g_final, tm_prompt, True)

    cos_s, slo_s, shi_s = _rope_tables(PAST_LEN + jnp.arange(1))
    bc = lambda t: jnp.broadcast_to(t, (n_seq, HEAD_DIM))
    q_s, k_s, v_s = _qkv_sample(y_s, row(norm_mix[1]), w_qkv_b, bc(cos_s), bc(slo_s), bc(shi_s))
    kmean_s = _kmean_sample(cache_k, page_table, 0)
    sel = _select_sample(q_s, kmean_s)
    heads = lambda t: t.reshape(n_seq, N_HEADS, 1, HEAD_DIM)
    attn_s = _attn_sample(heads(q_s), heads(k_s), heads(v_s), cache_k, cache_v,
                          page_table, sel.reshape(-1), 0)
    y_s = _post(y_s, attn_s.reshape(n_seq, D_MODEL), w_o_b, row(norm_ffn[1]),
                w_gu_b[1], w_down_b[1], g_final, n_seq, True)

    return (y_p.reshape(bsz, seq, D_MODEL),
            y_s.reshape(n_seq, 1, D_MODEL),
            conv_p[None],
            conv_s.transpose(1, 0, 2)[None],
            k_pages[None],
            v_pages[None],
            heads(k_s)[None],
            heads(v_s)[None])
```

```python
import functools
from typing import NamedTuple

import jax
import jax.numpy as jnp
from jax import lax
from jax.experimental import pallas as pl
from jax.experimental.pallas import tpu as pltpu

D_MODEL = 1024
N_HEADS = 8
HEAD_DIM = D_MODEL // N_HEADS
ROT_DIM = HEAD_DIM // 4
ROPE_THETA = 500000.0
MOBA_BLOCK = 256
MOBA_TOPK = 3
PAGE_SIZE = 128
PAGES_PER_BLOCK = MOBA_BLOCK // PAGE_SIZE
PAST_LEN = 16384
RMS_EPS = 1e-6
ATTN_SCALE = HEAD_DIM ** -0.5
LOG2_E = 1.4426950408889634

VMEM_LIMIT_BYTES = 56 * 1024 * 1024
CONV_PAD = 8
VT_ONES_ROWS = 16
ATTN_HEADS_PER_STEP = 4

F32 = jnp.float32
BF16 = jnp.bfloat16
NEG_INF = float("-inf")


def _rms(x, g):
    return x * lax.rsqrt(jnp.mean(x * x, axis=-1, keepdims=True) + RMS_EPS) * g


def _rope(x, cos, sin_lo, sin_hi):
    half = ROT_DIM // 2
    return (x * cos
            + pltpu.roll(x, HEAD_DIM - half, axis=1) * sin_lo
            + pltpu.roll(x, half, axis=1) * sin_hi)


def _rope_tables(pos):
    half = ROT_DIM // 2
    inv = jnp.power(F32(ROPE_THETA), -jnp.arange(half, dtype=F32) * 2.0 / ROT_DIM)
    ang = pos.astype(F32)[:, None] * inv[None, :]
    cos, sin = jnp.cos(ang), jnp.sin(ang)
    t = pos.shape[0]
    rest = HEAD_DIM - ROT_DIM
    cos_f = jnp.concatenate([cos, cos, jnp.ones((t, rest), F32)], axis=1)
    sin_lo = jnp.concatenate([-sin, jnp.zeros((t, half + rest), F32)], axis=1)
    sin_hi = jnp.concatenate([jnp.zeros((t, half), F32), sin, jnp.zeros((t, rest), F32)], axis=1)
    return cos_f, sin_lo, sin_hi


def _params(*semantics):
    return pltpu.CompilerParams(dimension_semantics=semantics,
                                vmem_limit_bytes=VMEM_LIMIT_BYTES)


def _full(shape):
    return pl.BlockSpec(shape, lambda *_: (0,) * len(shape))


def _page_specs(layer, first_page, pages_per_step, step_of):
    def spec(r):
        def index_map(*args):
            page = first_page + step_of(*args[:-1]) * pages_per_step + r
            return (layer, args[-1][page], 0, 0, 0)
        return pl.BlockSpec((1, 1, N_HEADS, PAGE_SIZE, HEAD_DIM), index_map)
    return [spec(r) for r in range(pages_per_step)]


def _kmean_spec(pages_per_step, step_of):
    blocks = pages_per_step // PAGES_PER_BLOCK
    return pl.BlockSpec((1, blocks, N_HEADS, HEAD_DIM),
                        lambda *args: (step_of(*args[:-1]), 0, 0, 0))


def _kmean_shape(n_steps, pages_per_step):
    return jax.ShapeDtypeStruct(
        (n_steps, pages_per_step // PAGES_PER_BLOCK, N_HEADS, HEAD_DIM), F32)


def _mean_block(page_refs, kmean_ref, j, zero):
    tot = zero + jnp.sum(page_refs[j * PAGES_PER_BLOCK][0, 0], axis=1)
    for r in range(1, PAGES_PER_BLOCK):
        tot = tot + jnp.sum(page_refs[j * PAGES_PER_BLOCK + r][0, 0], axis=1)
    mean = tot / MOBA_BLOCK
    kmean_ref[0, j] = mean
    return mean


def _zero_after(x):
    bits = lax.bitcast_convert_type(x, jnp.uint32)
    return lax.bitcast_convert_type((bits >> 31) >> 1, F32)


def _dot_with_page_means(x, x_ref, w_ref, n_chunks, page_refs, kmean_ref):
    n = w_ref.shape[1] // n_chunks
    n_blocks = len(page_refs) // PAGES_PER_BLOCK
    per_gap = -(-n_blocks // max(n_chunks - 1, 1))
    x_ref[...] = x
    start = _zero_after(x[0:N_HEADS, 0:HEAD_DIM].astype(F32))
    outs, j = [], 0
    for c in range(n_chunks):
        outs.append(jnp.dot(x_ref[...], w_ref[:, c * n:(c + 1) * n],
                            preferred_element_type=F32))
        tie = None
        while j < min(n_blocks, (c + 1) * per_gap):
            mean = _mean_block(page_refs, kmean_ref, j, start)
            tie = mean if tie is None else tie + mean
            j += 1
        if tie is not None:
            zero = _zero_after(jnp.concatenate([tie, tie], axis=0)).astype(x_ref.dtype)
            x_ref[0:2 * N_HEADS, 0:HEAD_DIM] = x_ref[0:2 * N_HEADS, 0:HEAD_DIM] + zero
    return outs


def _mix_prompt_kernel(*refs, n_pages):
    x_ref, g_ref, win_ref, wc_ref = refs[1:5]
    page_refs = refs[5:5 + n_pages]
    a_ref, st_ref, kmean_ref, uext_ref, h_ref = refs[5 + n_pages:]
    s = pl.program_id(1)
    tm = x_ref.shape[1]

    @pl.when(s == 0)
    def _():
        uext_ref[0:CONV_PAD, :] = jnp.zeros((CONV_PAD, D_MODEL), F32)

    h = _rms(x_ref[0], g_ref[...]).astype(BF16)
    parts = _dot_with_page_means(h, h_ref, win_ref, 6, page_refs, kmean_ref)
    b, c, xin = [jnp.concatenate(parts[2 * i:2 * i + 2], axis=1) for i in range(3)]
    u = c * xin
    uext_ref[CONV_PAD:CONV_PAD + tm, :] = u
    um1 = uext_ref[CONV_PAD - 1:CONV_PAD - 1 + tm, :]
    um2 = uext_ref[CONV_PAD - 2:CONV_PAD - 2 + tm, :]
    conv = wc_ref[0:1, :] * um2 + wc_ref[1:2, :] * um1 + wc_ref[2:3, :] * u
    a_ref[0] = (b * conv).astype(BF16)
    tail = u[tm - 2:tm, :]
    uext_ref[CONV_PAD - 2:CONV_PAD, :] = tail

    @pl.when(s == pl.num_programs(1) - 1)
    def _():
        st_ref[0] = tail


def _mix_prompt(x, g, w_in, w_conv, tm, pages):
    bsz, seq, _ = x.shape
    n_s = seq // tm
    step_of = lambda b, s: b * n_s + s
    return pl.pallas_call(
        functools.partial(_mix_prompt_kernel, n_pages=pages.per_step),
        grid_spec=pltpu.PrefetchScalarGridSpec(
            num_scalar_prefetch=1,
            grid=(bsz, n_s),
            in_specs=[
                pl.BlockSpec((1, tm, D_MODEL), lambda b, s, pt: (b, s, 0)),
                _full((1, D_MODEL)),
                _full((D_MODEL, 3 * D_MODEL)),
                _full((3, D_MODEL)),
            ] + _page_specs(pages.layer, pages.first, pages.per_step, step_of),
            out_specs=[
                pl.BlockSpec((1, tm, D_MODEL), lambda b, s, pt: (b, s, 0)),
                pl.BlockSpec((1, 2, D_MODEL), lambda b, s, pt: (b, 0, 0)),
                _kmean_spec(pages.per_step, step_of),
            ],
            scratch_shapes=[pltpu.VMEM((CONV_PAD + tm, D_MODEL), F32),
                            pltpu.VMEM((tm, D_MODEL), BF16)],
        ),
        out_shape=[
            jax.ShapeDtypeStruct((bsz, seq, D_MODEL), BF16),
            jax.ShapeDtypeStruct((bsz, 2, D_MODEL), F32),
            _kmean_shape(bsz * n_s, pages.per_step),
        ],
        compiler_params=_params("arbitrary", "arbitrary"),
        name="mix_prompt",
    )(pages.table, x, g, w_in, w_conv, *([pages.cache] * pages.per_step))


def _mix_sample_kernel(x_ref, st_ref, g_ref, win_ref, wc_ref, a_ref, nst_ref):
    h = _rms(x_ref[...], g_ref[...]).astype(BF16)
    bcx = jnp.dot(h, win_ref[...], preferred_element_type=F32)
    b = bcx[:, :D_MODEL]
    u = bcx[:, D_MODEL:2 * D_MODEL] * bcx[:, 2 * D_MODEL:]
    um2 = st_ref[0]
    um1 = st_ref[1]
    conv = wc_ref[0:1, :] * um2 + wc_ref[1:2, :] * um1 + wc_ref[2:3, :] * u
    a_ref[...] = (b * conv).astype(BF16)
    nst_ref[0] = um1
    nst_ref[1] = u


def _mix_sample(x, state_t, g, w_in, w_conv):
    n = x.shape[0]
    return pl.pallas_call(
        _mix_sample_kernel,
        grid=(1,),
        in_specs=[
            _full((n, D_MODEL)),
            _full((2, n, D_MODEL)),
            _full((1, D_MODEL)),
            _full((D_MODEL, 3 * D_MODEL)),
            _full((3, D_MODEL)),
        ],
        out_specs=[_full((n, D_MODEL)), _full((2, n, D_MODEL))],
        out_shape=[
            jax.ShapeDtypeStruct((n, D_MODEL), BF16),
            jax.ShapeDtypeStruct((2, n, D_MODEL), F32),
        ],
        compiler_params=_params("arbitrary"),
        name="mix_sample",
    )(x, state_t, g, w_in, w_conv)


def _post_kernel(*refs, final_norm, n_pages):
    if n_pages:
        refs = refs[1:]
    res_ref, a_ref, wpre_ref, g_ref, wgu_ref, wd_ref, gf_ref = refs[:7]
    y_ref = refs[7 + n_pages]
    y1 = res_ref[...] + jnp.dot(a_ref[...], wpre_ref[...], preferred_element_type=F32)
    h = _rms(y1, g_ref[...]).astype(BF16)
    kmean_ref = refs[8 + n_pages] if n_pages else None
    gate, up = _dot_with_page_means(h, refs[-1], wgu_ref, 2, refs[7:7 + n_pages], kmean_ref)
    act = (gate * (1.0 / (1.0 + jnp.exp(-gate))) * up).astype(BF16)
    y2 = y1 + jnp.dot(act, wd_ref[...], preferred_element_type=F32)
    if final_norm:
        y2 = _rms(y2, gf_ref[...])
    y_ref[...] = y2


def _post(res, a, w_pre, g_ffn, w_gu, w_down, g_final, tm, final_norm, pages=None):
    m = res.shape[0]
    d_ff = w_down.shape[0]
    tile = pl.BlockSpec((tm, D_MODEL), lambda i, *_: (i, 0))
    in_specs = [
        tile, tile,
        _full((D_MODEL, D_MODEL)),
        _full((1, D_MODEL)),
        _full((D_MODEL, 2 * d_ff)),
        _full((d_ff, D_MODEL)),
        _full((1, D_MODEL)),
    ]
    out_specs = [tile]
    out_shape = [jax.ShapeDtypeStruct((m, D_MODEL), F32)]
    operands = [res, a, w_pre, g_ffn, w_gu, w_down, g_final]
    n_pages = 0
    if pages is not None:
        n_pages = pages.per_step
        step_of = lambda i: i
        in_specs += _page_specs(pages.layer, pages.first, n_pages, step_of)
        out_specs.append(_kmean_spec(n_pages, step_of))
        out_shape.append(_kmean_shape(m // tm, n_pages))
        operands = [pages.table] + operands + [pages.cache] * n_pages
    out = pl.pallas_call(
        functools.partial(_post_kernel, final_norm=final_norm, n_pages=n_pages),
        grid_spec=pltpu.PrefetchScalarGridSpec(
            num_scalar_prefetch=1 if n_pages else 0,
            grid=(m // tm,),
            in_specs=in_specs,
            out_specs=out_specs,
            scratch_shapes=[pltpu.VMEM((tm, D_MODEL), BF16)],
        ),
        out_shape=out_shape,
        compiler_params=_params("arbitrary"),
        name="post_final" if final_norm else "post",
    )(*operands)
    return out if n_pages else out[0]


def _qkv_prompt_kernel(*refs, n_pages):
    x_ref, g_ref, w_ref, cos_ref, slo_ref, shi_ref = refs[1:7]
    page_refs = refs[7:7 + n_pages]
    kp_ref, vp_ref, qt_ref, kb_ref, vt_ref, km_ref, kmean_ref, h_ref = refs[7 + n_pages:]
    tm = x_ref.shape[1]
    h = _rms(x_ref[0], g_ref[...]).astype(BF16)
    parts = _dot_with_page_means(h, h_ref, w_ref, 6, page_refs, kmean_ref)
    q_all, k_all, v_all = [jnp.concatenate(parts[2 * i:2 * i + 2], axis=1) for i in range(3)]
    cos, slo, shi = cos_ref[...], slo_ref[...], shi_ref[...]
    for hd in range(N_HEADS):
        lo = hd * HEAD_DIM
        q = _rope(q_all[:, lo:lo + HEAD_DIM], cos, slo, shi)
        k = _rope(k_all[:, lo:lo + HEAD_DIM], cos, slo, shi)
        v = v_all[:, lo:lo + HEAD_DIM]
        for p in range(tm // PAGE_SIZE):
            kp_ref[0, p, hd] = k[p * PAGE_SIZE:(p + 1) * PAGE_SIZE]
            vp_ref[0, p, hd] = v[p * PAGE_SIZE:(p + 1) * PAGE_SIZE]
        for j in range(tm // MOBA_BLOCK):
            rows = slice(j * MOBA_BLOCK, (j + 1) * MOBA_BLOCK)
            qt_ref[0, hd, j] = q[rows].T
            kb_ref[0, hd, j] = k[rows].astype(BF16)
            vt_ref[0, hd, j, :HEAD_DIM] = v[rows].T.astype(BF16)
            vt_ref[0, hd, j, HEAD_DIM:] = jnp.ones((VT_ONES_ROWS, MOBA_BLOCK), BF16)
            km_ref[0, j, :, lo:lo + HEAD_DIM] = jnp.mean(k[rows], axis=0, keepdims=True)


def _qkv_prompt(x, g, w_qkv, cos, slo, shi, tm, pages):
    bsz, seq, _ = x.shape
    n_pages, n_blk = seq // PAGE_SIZE, seq // MOBA_BLOCK
    tp, tb = tm // PAGE_SIZE, tm // MOBA_BLOCK
    n_s = seq // tm
    step_of = lambda b, s: b * n_s + s
    tab = pl.BlockSpec((tm, HEAD_DIM), lambda b, s, pt: (s, 0))
    by_page = lambda b, s, pt: (b, s, 0, 0, 0)
    by_block = lambda b, s, pt: (b, 0, s, 0, 0)
    return pl.pallas_call(
        functools.partial(_qkv_prompt_kernel, n_pages=pages.per_step),
        grid_spec=pltpu.PrefetchScalarGridSpec(
            num_scalar_prefetch=1,
            grid=(bsz, n_s),
            in_specs=[
                pl.BlockSpec((1, tm, D_MODEL), lambda b, s, pt: (b, s, 0)),
                _full((1, D_MODEL)),
                _full((D_MODEL, 3 * D_MODEL)),
                tab, tab, tab,
            ] + _page_specs(pages.layer, pages.first, pages.per_step, step_of),
            out_specs=[
                pl.BlockSpec((1, tp, N_HEADS, PAGE_SIZE, HEAD_DIM), by_page),
                pl.BlockSpec((1, tp, N_HEADS, PAGE_SIZE, HEAD_DIM), by_page),
                pl.BlockSpec((1, N_HEADS, tb, HEAD_DIM, MOBA_BLOCK), by_block),
                pl.BlockSpec((1, N_HEADS, tb, MOBA_BLOCK, HEAD_DIM), by_block),
                pl.BlockSpec((1, N_HEADS, tb, HEAD_DIM + VT_ONES_ROWS, MOBA_BLOCK), by_block),
                pl.BlockSpec((1, tb, 1, D_MODEL), lambda b, s, pt: (b, s, 0, 0)),
                _kmean_spec(pages.per_step, step_of),
            ],
            scratch_shapes=[pltpu.VMEM((tm, D_MODEL), BF16)],
        ),
        out_shape=[
            jax.ShapeDtypeStruct((bsz, n_pages, N_HEADS, PAGE_SIZE, HEAD_DIM), F32),
            jax.ShapeDtypeStruct((bsz, n_pages, N_HEADS, PAGE_SIZE, HEAD_DIM), F32),
            jax.ShapeDtypeStruct((bsz, N_HEADS, n_blk, HEAD_DIM, MOBA_BLOCK), F32),
            jax.ShapeDtypeStruct((bsz, N_HEADS, n_blk, MOBA_BLOCK, HEAD_DIM), BF16),
            jax.ShapeDtypeStruct((bsz, N_HEADS, n_blk, HEAD_DIM + VT_ONES_ROWS, MOBA_BLOCK), BF16),
            jax.ShapeDtypeStruct((bsz, n_blk, 1, D_MODEL), F32),
            _kmean_shape(bsz * n_s, pages.per_step),
        ],
        compiler_params=_params("arbitrary", "arbitrary"),
        name="qkv_prompt",
    )(pages.table, x, g, w_qkv, cos, slo, shi, *([pages.cache] * pages.per_step))


def _qkv_sample_kernel(x_ref, g_ref, w_ref, cos_ref, slo_ref, shi_ref, q_ref, k_ref, v_ref):
    h = _rms(x_ref[...], g_ref[...]).astype(BF16)
    qkv = jnp.dot(h, w_ref[...], preferred_element_type=F32)
    cos, slo, shi = cos_ref[...], slo_ref[...], shi_ref[...]
    for hd in range(N_HEADS):
        lo = hd * HEAD_DIM
        q_ref[:, lo:lo + HEAD_DIM] = _rope(qkv[:, lo:lo + HEAD_DIM], cos, slo, shi)
        k_ref[:, lo:lo + HEAD_DIM] = _rope(
            qkv[:, D_MODEL + lo:D_MODEL + lo + HEAD_DIM], cos, slo, shi)
    v_ref[...] = qkv[:, 2 * D_MODEL:]


def _qkv_sample(x, g, w_qkv, cos, slo, shi):
    n = x.shape[0]
    out = jax.ShapeDtypeStruct((n, D_MODEL), F32)
    return pl.pallas_call(
        _qkv_sample_kernel,
        grid=(1,),
        in_specs=[
            _full((n, D_MODEL)),
            _full((1, D_MODEL)),
            _full((D_MODEL, 3 * D_MODEL)),
            _full((n, HEAD_DIM)), _full((n, HEAD_DIM)), _full((n, HEAD_DIM)),
        ],
        out_specs=[_full((n, D_MODEL))] * 3,
        out_shape=[out, out, out],
        compiler_params=_params("arbitrary"),
        name="qkv_sample",
    )(x, g, w_qkv, cos, slo, shi)


def _attn_prompt_kernel(qt_ref, k_ref, vt_ref, km_ref, o_ref,
                        bias_ref, qs_ref, m_ref, mb_ref, alpha_ref, p_ref, s_ref, acc_ref):
    qb = pl.program_id(2)
    n_heads = qt_ref.shape[1]
    n_blk = km_ref.shape[2]
    tq = qt_ref.shape[4]
    heads = range(n_heads)
    blk = lax.broadcasted_iota(jnp.int32, (n_blk, tq), 0)
    past = blk < qb
    key = lax.broadcasted_iota(jnp.int32, (tq, tq), 0)
    qry = lax.broadcasted_iota(jnp.int32, (tq, tq), 1)
    causal = jnp.where(key <= qry, 0.0, NEG_INF)

    def score_stage(j, slot):
        scores = [jnp.dot(k_ref[0, h, j], qs_ref[h], preferred_element_type=F32)
                  for h in heads]
        for h in heads:
            s = scores[h] + bias_ref[h, pl.ds(j, 1), :]
            s_ref[slot, h] = s
            mb_ref[slot, h] = jnp.max(s, axis=0, keepdims=True)

    def value_stage(j, slot):
        pvs = [jnp.dot(vt_ref[0, h, j], p_ref[slot, h], preferred_element_type=F32)
               for h in heads]
        for h in heads:
            acc_ref[h] = alpha_ref[slot, h] * acc_ref[h] + pvs[h]

    gates = [lax.dot_general(km_ref[0, h], qt_ref[0, h, 0], (((1,), (0,)), ((), ())),
                             precision=lax.Precision.HIGHEST,
                             preferred_element_type=F32) for h in heads]
    for h in heads:
        qs_ref[h] = (qt_ref[0, h, 0] * (ATTN_SCALE * LOG2_E)).astype(BF16)
    own = [jnp.dot(k_ref[0, h, qb], qs_ref[h], preferred_element_type=F32) for h in heads]
    for h in heads:
        gate = jnp.where(past, gates[h], NEG_INF)
        rank = jnp.zeros((n_blk, tq), F32)
        for jp in range(n_blk):
            row = gate[jp:jp + 1, :]
            tie = jnp.where(blk > jp, 1.0, 0.0)
            rank = rank + jnp.where(row > gate, 1.0, jnp.where(row == gate, tie, 0.0))
        bias_ref[h] = jnp.where(rank < MOBA_TOPK, jnp.where(past, 0.0, NEG_INF), NEG_INF)
    for h in heads:
        s = own[h] + causal
        m = jnp.max(s, axis=0, keepdims=True)
        m_ref[h] = m
        alpha_ref[1, h] = jnp.ones((1, tq), F32)
        p_ref[1, h] = jnp.exp2(s - m).astype(BF16)
        acc_ref[h] = jnp.zeros(acc_ref.shape[1:], F32)
    score_stage(0, 0)

    def trip(j, cur):
        nxt = 1 - cur
        value_stage(jnp.where(j == 0, qb, j - 1), nxt)
        for h in heads:
            m_old = m_ref[h]
            m_new = jnp.maximum(m_old, mb_ref[cur, h])
            m_ref[h] = m_new
            alpha_ref[cur, h] = jnp.exp2(m_old - m_new)
            p_ref[cur, h] = jnp.exp2(s_ref[cur, h] - m_new).astype(BF16)
        score_stage(jnp.minimum(j + 1, n_blk - 1), nxt)

    def body(i, carry):
        trip(2 * i, 0)

        @pl.when(2 * i + 1 < qb)
        def _():
            trip(2 * i + 1, 1)

        return carry

    lax.fori_loop(0, (qb + 1) // 2, body, 0)
    value_stage(jnp.where(qb == 0, qb, qb - 1), lax.rem(qb + 1, 2))
    for h in heads:
        o = acc_ref[h, :HEAD_DIM] * (1.0 / acc_ref[h, HEAD_DIM:HEAD_DIM + 1])
        o_ref[0, :, h * HEAD_DIM:(h + 1) * HEAD_DIM] = o.T.astype(BF16)


def _attn_prompt(qt, kb, vt, km):
    bsz, _, n_blk, _, tq = qt.shape
    vt_rows = vt.shape[3]
    hps = ATTN_HEADS_PER_STEP
    return pl.pallas_call(
        _attn_prompt_kernel,
        grid=(bsz, N_HEADS // hps, n_blk),
        in_specs=[
            pl.BlockSpec((1, hps, 1, HEAD_DIM, tq), lambda b, h, i: (b, h, i, 0, 0)),
            pl.BlockSpec((1, hps, n_blk, tq, HEAD_DIM), lambda b, h, i: (b, h, 0, 0, 0)),
            pl.BlockSpec((1, hps, n_blk, vt_rows, tq), lambda b, h, i: (b, h, 0, 0, 0)),
            pl.BlockSpec((1, hps, n_blk, HEAD_DIM), lambda b, h, i: (b, h, 0, 0)),
        ],
        out_specs=pl.BlockSpec((1, tq, hps * HEAD_DIM), lambda b, h, i: (b, i, h)),
        out_shape=jax.ShapeDtypeStruct((bsz, n_blk * tq, D_MODEL), BF16),
        scratch_shapes=[
            pltpu.VMEM((hps, n_blk, tq), F32),
            pltpu.VMEM((hps, HEAD_DIM, tq), BF16),
            pltpu.VMEM((hps, 1, tq), F32),
            pltpu.VMEM((2, hps, 1, tq), F32),
            pltpu.VMEM((2, hps, 1, tq), F32),
            pltpu.VMEM((2, hps, tq, tq), BF16),
            pltpu.VMEM((2, hps, tq, tq), F32),
            pltpu.VMEM((hps, vt_rows, tq), F32),
        ],
        compiler_params=_params("arbitrary", "arbitrary", "arbitrary"),
        name="attn_prompt",
    )(qt, kb, vt, km)


def _select_sample_kernel(q_ref, km_ref, sel_ref):
    n_blk = km_ref.shape[1]
    prod = km_ref[0] * q_ref[0]
    gate = jnp.concatenate(
        [jnp.sum(prod[:, h * HEAD_DIM:(h + 1) * HEAD_DIM], axis=1, keepdims=True)
         for h in range(N_HEADS)], axis=1)
    blk = lax.broadcasted_iota(jnp.int32, gate.shape, 0)
    picks = []
    for _ in range(MOBA_TOPK):
        best = jnp.max(gate, axis=0, keepdims=True)
        idx = jnp.min(jnp.where(gate == best, blk, n_blk), axis=0, keepdims=True)
        picks.append(idx)
        gate = jnp.where(blk == idx, NEG_INF, gate)
    sel_ref[0] = jnp.concatenate(picks, axis=0)


def _select_sample(q, kmean):
    n_seq, n_blk = kmean.shape[0], kmean.shape[1]
    return pl.pallas_call(
        _select_sample_kernel,
        grid=(n_seq,),
        in_specs=[
            pl.BlockSpec((1, 1, D_MODEL), lambda b: (b, 0, 0)),
            pl.BlockSpec((1, n_blk, D_MODEL), lambda b: (b, 0, 0)),
        ],
        out_specs=pl.BlockSpec((1, MOBA_TOPK, N_HEADS), lambda b: (b, 0, 0)),
        out_shape=jax.ShapeDtypeStruct((n_seq, MOBA_TOPK, N_HEADS), jnp.int32),
        compiler_params=_params("arbitrary"),
        name="select_sample",
    )(q.reshape(n_seq, 1, D_MODEL), kmean.reshape(n_seq, n_blk, D_MODEL))


def _attn_sample_kernel(pt_ref, sel_ref, q_ref, kn_ref, vn_ref, *refs):
    del pt_ref, sel_ref
    n_sel_pages = (len(refs) - 1) // 2
    k_refs, v_refs, o_ref = refs[:n_sel_pages], refs[n_sel_pages:-1], refs[-1]
    q = q_ref[0, 0]
    s_new = jnp.sum(q * kn_ref[0, 0], axis=1, keepdims=True) * ATTN_SCALE
    scores = [jnp.sum(k_ref[0, 0, 0] * q, axis=1, keepdims=True) * ATTN_SCALE
              for k_ref in k_refs]
    m = s_new
    for s in scores:
        m = jnp.maximum(m, jnp.max(s, axis=0, keepdims=True))
    p_new = jnp.exp(s_new - m)
    l = p_new
    acc = p_new * vn_ref[0, 0]
    for s, v_ref in zip(scores, v_refs):
        p = jnp.exp(s - m)
        l = l + jnp.sum(p, axis=0, keepdims=True)
        acc = acc + jnp.sum(p * v_ref[0, 0, 0], axis=0, keepdims=True)
    o_ref[0, 0] = (acc * (1.0 / l)).astype(o_ref.dtype)


def _attn_sample(q, kn, vn, cache_k, cache_v, page_table, sel, layer):
    n_seq = q.shape[0]
    page_block = (1, 1, 1, PAGE_SIZE, HEAD_DIM)

    def page_spec(n, r):
        def index_map(b, h, pt, sl):
            blk = sl[(b * MOBA_TOPK + n) * N_HEADS + h]
            return (layer, pt[b, blk * PAGES_PER_BLOCK + r], h, 0, 0)
        return pl.BlockSpec(page_block, index_map)

    row = pl.BlockSpec((1, 1, 1, HEAD_DIM), lambda b, h, pt, sl: (b, h, 0, 0))
    pages = [page_spec(n, r) for n in range(MOBA_TOPK) for r in range(PAGES_PER_BLOCK)]
    return pl.pallas_call(
        _attn_sample_kernel,
        grid_spec=pltpu.PrefetchScalarGridSpec(
            num_scalar_prefetch=2,
            grid=(n_seq, N_HEADS),
            in_specs=[row, row, row] + pages + pages,
            out_specs=row,
        ),
        out_shape=jax.ShapeDtypeStruct((n_seq, N_HEADS, 1, HEAD_DIM), BF16),
        compiler_params=_params("arbitrary", "arbitrary"),
        name="attn_sample",
    )(page_table, sel, q, kn, vn, *([cache_k] * len(pages)), *([cache_v] * len(pages)))


class _Pages(NamedTuple):
    cache: jax.Array
    table: jax.Array
    layer: int
    first: int
    per_step: int


class _Plan(NamedTuple):
    tm_mix: int
    tm_post: int
    tm_qkv: int
    quarter: int

    def pages(self, cache, table, layer, idx, n_steps):
        per_step = self.quarter // n_steps
        assert per_step * n_steps == self.quarter and per_step % PAGES_PER_BLOCK == 0
        return _Pages(cache, table, layer, idx * self.quarter, per_step)


def _plan(n_cache_pages):
    assert n_cache_pages % 4 == 0
    return _Plan(tm_mix=512, tm_post=256, tm_qkv=512, quarter=n_cache_pages // 4)


def kernel(x_prompt, x_sample, state_conv, cache_k, cache_v, page_table,
           norm_mix, norm_ffn, norm_final, w_conv_in, w_conv, w_conv_out,
           w_qkv, w_o, w_gate_up, w_down):
    bsz, seq, _ = x_prompt.shape
    n_seq, n_pages = page_table.shape
    n_rows = bsz * seq
    plan = _plan(n_seq * n_pages)
    table = page_table.reshape(-1)
    row = lambda v: v.reshape(1, D_MODEL)
    w_in_b = w_conv_in[0].astype(BF16)
    w_out_b = w_conv_out[0].astype(BF16)
    w_qkv_b = w_qkv[0].astype(BF16)
    w_o_b = w_o[0].astype(BF16)
    w_gu_b = w_gate_up.astype(BF16)
    w_down_b = w_down.astype(BF16)
    g_final = row(norm_final)

    a_p, conv_p, kmean_0 = _mix_prompt(
        x_prompt, row(norm_mix[0]), w_in_b, w_conv[0], plan.tm_mix,
        plan.pages(cache_k, table, 0, 0, n_rows // plan.tm_mix))
    y_p, kmean_1 = _post(
        x_prompt.reshape(n_rows, D_MODEL), a_p.reshape(n_rows, D_MODEL),
        w_out_b, row(norm_ffn[0]), w_gu_b[0], w_down_b[0], g_final, plan.tm_post, False,
        plan.pages(cache_k, table, 0, 1, n_rows // plan.tm_post))

    xs = x_sample.reshape(n_seq, D_MODEL)
    a_s, conv_s = _mix_sample(xs, state_conv[0].transpose(1, 0, 2), row(norm_mix[0]),
                              w_in_b, w_conv[0])
    y_s = _post(xs, a_s, w_out_b, row(norm_ffn[0]), w_gu_b[0], w_down_b[0], g_final,
                n_seq, False)

    cos, slo, shi = _rope_tables(jnp.arange(seq))
    k_pages, v_pages, qt, kb, vt, km, kmean_2 = _qkv_prompt(
        y_p.reshape(bsz, seq, D_MODEL), row(norm_mix[1]), w_qkv_b, cos, slo, shi, plan.tm_qkv,
        plan.pages(cache_k, table, 0, 2, n_rows // plan.tm_qkv))
    n_blk = seq // MOBA_BLOCK
    km_h = km.reshape(bsz, n_blk, N_HEADS, HEAD_DIM).transpose(0, 2, 1, 3)
    attn_p = _attn_prompt(qt, kb, vt, km_h)
    y_p, kmean_3 = _post(
        y_p, attn_p.reshape(n_rows, D_MODEL), w_o_b, row(norm_ffn[1]),
        w_gu_b[1], w_down_b[1], g_final, plan.tm_post, True,
        plan.pages(cache_k, table, 0, 3, n_rows // plan.tm_post))

    cos_s, slo_s, shi_s = _rope_tables(PAST_LEN + jnp.arange(1))
    bc = lambda t: jnp.broadcast_to(t, (n_seq, HEAD_DIM))
    q_s, k_s, v_s = _qkv_sample(y_s, row(norm_mix[1]), w_qkv_b, bc(cos_s), bc(slo_s), bc(shi_s))
    kmean_s = jnp.concatenate(
        [part.reshape(-1, N_HEADS, HEAD_DIM) for part in (kmean_0, kmean_1, kmean_2, kmean_3)],
        axis=0).reshape(n_seq, n_pages // PAGES_PER_BLOCK, N_HEADS, HEAD_DIM)
    sel = _select_sample(q_s, kmean_s)
    heads = lambda t: t.reshape(n_seq, N_HEADS, 1, HEAD_DIM)
    attn_s = _attn_sample(heads(q_s), heads(k_s), heads(v_s), cache_k, cache_v,
                          page_table, sel.reshape(-1), 0)
    y_s = _post(y_s, attn_s.reshape(n_seq, D_MODEL), w_o_b, row(norm_ffn[1]),
                w_gu_b[1], w_down_b[1], g_final, n_seq, True)

    return (y_p.reshape(bsz, seq, D_MODEL),
            y_s.reshape(n_seq, 1, D_MODEL),
            conv_p[None],
            conv_s.transpose(1, 0, 2)[None],
            k_pages[None],
            v_pages[None],
            heads(k_s)[None],
            heads(v_s)[None])
```

```python
import functools
from typing import NamedTuple

import jax
import jax.numpy as jnp
from jax import lax
from jax.experimental import pallas as pl
from jax.experimental.pallas import tpu as pltpu

D_MODEL = 1024
N_HEADS = 8
HEAD_DIM = D_MODEL // N_HEADS
ROT_DIM = HEAD_DIM // 4
ROPE_THETA = 500000.0
MOBA_BLOCK = 256
MOBA_TOPK = 3
PAGE_SIZE = 128
PAGES_PER_BLOCK = MOBA_BLOCK // PAGE_SIZE
PAST_LEN = 16384
RMS_EPS = 1e-6
ATTN_SCALE = HEAD_DIM ** -0.5
LOG2_E = 1.4426950408889634

VMEM_LIMIT_BYTES = 56 * 1024 * 1024
CONV_PAD = 8
VT_ONES_ROWS = 16
POST_GROUP_ROWS = 256
ATTN_HEADS_PER_STEP = 4

F32 = jnp.float32
BF16 = jnp.bfloat16
NEG_INF = float("-inf")


def _rms(x, g):
    return x * lax.rsqrt(jnp.mean(x * x, axis=-1, keepdims=True) + RMS_EPS) * g


def _rope(x, cos, sin_lo, sin_hi):
    half = ROT_DIM // 2
    return (x * cos
            + pltpu.roll(x, HEAD_DIM - half, axis=1) * sin_lo
            + pltpu.roll(x, half, axis=1) * sin_hi)


def _rope_tables(pos):
    half = ROT_DIM // 2
    inv = jnp.power(F32(ROPE_THETA), -jnp.arange(half, dtype=F32) * 2.0 / ROT_DIM)
    ang = pos.astype(F32)[:, None] * inv[None, :]
    cos, sin = jnp.cos(ang), jnp.sin(ang)
    t = pos.shape[0]
    rest = HEAD_DIM - ROT_DIM
    cos_f = jnp.concatenate([cos, cos, jnp.ones((t, rest), F32)], axis=1)
    sin_lo = jnp.concatenate([-sin, jnp.zeros((t, half + rest), F32)], axis=1)
    sin_hi = jnp.concatenate([jnp.zeros((t, half), F32), sin, jnp.zeros((t, rest), F32)], axis=1)
    return cos_f, sin_lo, sin_hi


def _params(*semantics):
    return pltpu.CompilerParams(dimension_semantics=semantics,
                                vmem_limit_bytes=VMEM_LIMIT_BYTES)


def _full(shape):
    return pl.BlockSpec(shape, lambda *_: (0,) * len(shape))


def _page_specs(layer, first_page, pages_per_step, step_of):
    def spec(r):
        def index_map(*args):
            page = first_page + step_of(*args[:-1]) * pages_per_step + r
            return (layer, args[-1][page], 0, 0, 0)
        return pl.BlockSpec((1, 1, N_HEADS, PAGE_SIZE, HEAD_DIM), index_map)
    return [spec(r) for r in range(pages_per_step)]


def _kmean_spec(pages_per_step, step_of):
    blocks = pages_per_step // PAGES_PER_BLOCK
    return pl.BlockSpec((1, blocks, N_HEADS, HEAD_DIM),
                        lambda *args: (step_of(*args[:-1]), 0, 0, 0))


def _kmean_shape(n_steps, pages_per_step):
    return jax.ShapeDtypeStruct(
        (n_steps, pages_per_step // PAGES_PER_BLOCK, N_HEADS, HEAD_DIM), F32)


def _mean_block(page_refs, kmean_ref, j, zero):
    tot = zero + jnp.sum(page_refs[j * PAGES_PER_BLOCK][0, 0], axis=1)
    for r in range(1, PAGES_PER_BLOCK):
        tot = tot + jnp.sum(page_refs[j * PAGES_PER_BLOCK + r][0, 0], axis=1)
    mean = tot / MOBA_BLOCK
    kmean_ref[0, j] = mean
    return mean


def _zero_after(x):
    bits = lax.bitcast_convert_type(x, jnp.uint32)
    return lax.bitcast_convert_type((bits >> 31) >> 1, F32)


def _dot_with_page_means(x, x_ref, w_ref, n_chunks, page_refs, kmean_ref):
    n = w_ref.shape[1] // n_chunks
    n_blocks = len(page_refs) // PAGES_PER_BLOCK
    per_gap = -(-n_blocks // max(n_chunks - 1, 1))
    x_ref[...] = x
    start = _zero_after(x[0:N_HEADS, 0:HEAD_DIM].astype(F32))
    outs, j = [], 0
    for c in range(n_chunks):
        outs.append(jnp.dot(x_ref[...], w_ref[:, c * n:(c + 1) * n],
                            preferred_element_type=F32))
        tie = None
        while j < min(n_blocks, (c + 1) * per_gap):
            mean = _mean_block(page_refs, kmean_ref, j, start)
            tie = mean if tie is None else tie + mean
            j += 1
        if tie is not None:
            zero = _zero_after(jnp.concatenate([tie, tie], axis=0)).astype(x_ref.dtype)
            x_ref[0:2 * N_HEADS, 0:HEAD_DIM] = x_ref[0:2 * N_HEADS, 0:HEAD_DIM] + zero
    return outs


def _mix_prompt_kernel(*refs, n_pages):
    x_ref, g_ref, win_ref, wc_ref = refs[1:5]
    page_refs = refs[5:5 + n_pages]
    a_ref, st_ref, kmean_ref, uext_ref, h_ref = refs[5 + n_pages:]
    s = pl.program_id(1)
    tm = x_ref.shape[1]

    @pl.when(s == 0)
    def _():
        uext_ref[0:CONV_PAD, :] = jnp.zeros((CONV_PAD, D_MODEL), F32)

    h = _rms(x_ref[0], g_ref[...]).astype(BF16)
    parts = _dot_with_page_means(h, h_ref, win_ref, 6, page_refs, kmean_ref)
    b, c, xin = [jnp.concatenate(parts[2 * i:2 * i + 2], axis=1) for i in range(3)]
    u = c * xin
    uext_ref[CONV_PAD:CONV_PAD + tm, :] = u
    um1 = uext_ref[CONV_PAD - 1:CONV_PAD - 1 + tm, :]
    um2 = uext_ref[CONV_PAD - 2:CONV_PAD - 2 + tm, :]
    conv = wc_ref[0:1, :] * um2 + wc_ref[1:2, :] * um1 + wc_ref[2:3, :] * u
    a_ref[0] = (b * conv).astype(BF16)
    tail = u[tm - 2:tm, :]
    uext_ref[CONV_PAD - 2:CONV_PAD, :] = tail

    @pl.when(s == pl.num_programs(1) - 1)
    def _():
        st_ref[0] = tail


def _mix_prompt(x, g, w_in, w_conv, tm, pages):
    bsz, seq, _ = x.shape
    n_s = seq // tm
    step_of = lambda b, s: b * n_s + s
    return pl.pallas_call(
        functools.partial(_mix_prompt_kernel, n_pages=pages.per_step),
        grid_spec=pltpu.PrefetchScalarGridSpec(
            num_scalar_prefetch=1,
            grid=(bsz, n_s),
            in_specs=[
                pl.BlockSpec((1, tm, D_MODEL), lambda b, s, pt: (b, s, 0)),
                _full((1, D_MODEL)),
                _full((D_MODEL, 3 * D_MODEL)),
                _full((3, D_MODEL)),
            ] + _page_specs(pages.layer, pages.first, pages.per_step, step_of),
            out_specs=[
                pl.BlockSpec((1, tm, D_MODEL), lambda b, s, pt: (b, s, 0)),
                pl.BlockSpec((1, 2, D_MODEL), lambda b, s, pt: (b, 0, 0)),
                _kmean_spec(pages.per_step, step_of),
            ],
            scratch_shapes=[pltpu.VMEM((CONV_PAD + tm, D_MODEL), F32),
                            pltpu.VMEM((tm, D_MODEL), BF16)],
        ),
        out_shape=[
            jax.ShapeDtypeStruct((bsz, seq, D_MODEL), BF16),
            jax.ShapeDtypeStruct((bsz, 2, D_MODEL), F32),
            _kmean_shape(bsz * n_s, pages.per_step),
        ],
        compiler_params=_params("arbitrary", "arbitrary"),
        name="mix_prompt",
    )(pages.table, x, g, w_in, w_conv, *([pages.cache] * pages.per_step))


def _mix_sample_kernel(x_ref, st_ref, g_ref, win_ref, wc_ref, a_ref, nst_ref):
    h = _rms(x_ref[...], g_ref[...]).astype(BF16)
    bcx = jnp.dot(h, win_ref[...], preferred_element_type=F32)
    b = bcx[:, :D_MODEL]
    u = bcx[:, D_MODEL:2 * D_MODEL] * bcx[:, 2 * D_MODEL:]
    um2 = st_ref[0]
    um1 = st_ref[1]
    conv = wc_ref[0:1, :] * um2 + wc_ref[1:2, :] * um1 + wc_ref[2:3, :] * u
    a_ref[...] = (b * conv).astype(BF16)
    nst_ref[0] = um1
    nst_ref[1] = u


def _mix_sample(x, state_t, g, w_in, w_conv):
    n = x.shape[0]
    return pl.pallas_call(
        _mix_sample_kernel,
        grid=(1,),
        in_specs=[
            _full((n, D_MODEL)),
            _full((2, n, D_MODEL)),
            _full((1, D_MODEL)),
            _full((D_MODEL, 3 * D_MODEL)),
            _full((3, D_MODEL)),
        ],
        out_specs=[_full((n, D_MODEL)), _full((2, n, D_MODEL))],
        out_shape=[
            jax.ShapeDtypeStruct((n, D_MODEL), BF16),
            jax.ShapeDtypeStruct((2, n, D_MODEL), F32),
        ],
        compiler_params=_params("arbitrary"),
        name="mix_sample",
    )(x, state_t, g, w_in, w_conv)


def _post_kernel(*refs, final_norm, n_pages):
    if n_pages:
        refs = refs[1:]
    res_ref, a_ref, wpre_ref, g_ref, wgu_ref, wd_ref, gf_ref = refs[:7]
    y_ref = refs[7 + n_pages]
    kmean_ref = refs[8 + n_pages] if n_pages else None
    h_ref = refs[-1]
    tm = res_ref.shape[0]
    n_groups = 2 if tm >= 2 * POST_GROUP_ROWS else 1
    rows = [slice(r * tm // n_groups, (r + 1) * tm // n_groups) for r in range(n_groups)]
    y1 = [res_ref[r, :] + jnp.dot(a_ref[r, :], wpre_ref[...], preferred_element_type=F32)
          for r in rows]
    h = [_rms(y, g_ref[...]).astype(BF16) for y in y1]
    gu = []
    for i, r in enumerate(rows):
        pages = refs[7:7 + n_pages] if i == 0 else ()
        gu.append(_dot_with_page_means(h[i], h_ref.at[r, :], wgu_ref, 2, pages, kmean_ref))
    act = [(gate * (1.0 / (1.0 + jnp.exp(-gate))) * up).astype(BF16) for gate, up in gu]
    for i, r in enumerate(rows):
        y2 = y1[i] + jnp.dot(act[i], wd_ref[...], preferred_element_type=F32)
        if final_norm:
            y2 = _rms(y2, gf_ref[...])
        y_ref[r, :] = y2


def _post(res, a, w_pre, g_ffn, w_gu, w_down, g_final, tm, final_norm, pages=None):
    m = res.shape[0]
    d_ff = w_down.shape[0]
    tile = pl.BlockSpec((tm, D_MODEL), lambda i, *_: (i, 0))
    in_specs = [
        tile, tile,
        _full((D_MODEL, D_MODEL)),
        _full((1, D_MODEL)),
        _full((D_MODEL, 2 * d_ff)),
        _full((d_ff, D_MODEL)),
        _full((1, D_MODEL)),
    ]
    out_specs = [tile]
    out_shape = [jax.ShapeDtypeStruct((m, D_MODEL), F32)]
    operands = [res, a, w_pre, g_ffn, w_gu, w_down, g_final]
    n_pages = 0
    if pages is not None:
        n_pages = pages.per_step
        step_of = lambda i: i
        in_specs += _page_specs(pages.layer, pages.first, n_pages, step_of)
        out_specs.append(_kmean_spec(n_pages, step_of))
        out_shape.append(_kmean_shape(m // tm, n_pages))
        operands = [pages.table] + operands + [pages.cache] * n_pages
    out = pl.pallas_call(
        functools.partial(_post_kernel, final_norm=final_norm, n_pages=n_pages),
        grid_spec=pltpu.PrefetchScalarGridSpec(
            num_scalar_prefetch=1 if n_pages else 0,
            grid=(m // tm,),
            in_specs=in_specs,
            out_specs=out_specs,
            scratch_shapes=[pltpu.VMEM((tm, D_MODEL), BF16)],
        ),
        out_shape=out_shape,
        compiler_params=_params("arbitrary"),
        name="post_final" if final_norm else "post",
    )(*operands)
    return out if n_pages else out[0]


def _qkv_prompt_kernel(x_ref, g_ref, w_ref, cos_ref, slo_ref, shi_ref,
                       kp_ref, vp_ref, qt_ref, kb_ref, vt_ref, km_ref):
    tm = x_ref.shape[1]
    h = _rms(x_ref[0], g_ref[...]).astype(BF16)
    qkv = jnp.dot(h, w_ref[...], preferred_element_type=F32)
    cos, slo, shi = cos_ref[...], slo_ref[...], shi_ref[...]
    for hd in range(N_HEADS):
        lo = hd * HEAD_DIM
        q = _rope(qkv[:, lo:lo + HEAD_DIM], cos, slo, shi)
        k = _rope(qkv[:, D_MODEL + lo:D_MODEL + lo + HEAD_DIM], cos, slo, shi)
        v = qkv[:, 2 * D_MODEL + lo:2 * D_MODEL + lo + HEAD_DIM]
        for p in range(tm // PAGE_SIZE):
            kp_ref[0, p, hd] = k[p * PAGE_SIZE:(p + 1) * PAGE_SIZE]
            vp_ref[0, p, hd] = v[p * PAGE_SIZE:(p + 1) * PAGE_SIZE]
        for j in range(tm // MOBA_BLOCK):
            rows = slice(j * MOBA_BLOCK, (j + 1) * MOBA_BLOCK)
            qt_ref[0, hd, j] = q[rows].T
            kb_ref[0, hd, j] = k[rows].astype(BF16)
            vt_ref[0, hd, j, :HEAD_DIM] = v[rows].T.astype(BF16)
            vt_ref[0, hd, j, HEAD_DIM:] = jnp.ones((VT_ONES_ROWS, MOBA_BLOCK), BF16)
            km_ref[0, j, :, lo:lo + HEAD_DIM] = jnp.mean(k[rows], axis=0, keepdims=True)


def _qkv_prompt(x, g, w_qkv, cos, slo, shi, tm):
    bsz, seq, _ = x.shape
    n_pages, n_blk = seq // PAGE_SIZE, seq // MOBA_BLOCK
    tp, tb = tm // PAGE_SIZE, tm // MOBA_BLOCK
    tab = pl.BlockSpec((tm, HEAD_DIM), lambda b, s: (s, 0))
    by_page = lambda b, s: (b, s, 0, 0, 0)
    by_block = lambda b, s: (b, 0, s, 0, 0)
    return pl.pallas_call(
        _qkv_prompt_kernel,
        grid=(bsz, seq // tm),
        in_specs=[
            pl.BlockSpec((1, tm, D_MODEL), lambda b, s: (b, s, 0)),
            _full((1, D_MODEL)),
            _full((D_MODEL, 3 * D_MODEL)),
            tab, tab, tab,
        ],
        out_specs=[
            pl.BlockSpec((1, tp, N_HEADS, PAGE_SIZE, HEAD_DIM), by_page),
            pl.BlockSpec((1, tp, N_HEADS, PAGE_SIZE, HEAD_DIM), by_page),
            pl.BlockSpec((1, N_HEADS, tb, HEAD_DIM, MOBA_BLOCK), by_block),
            pl.BlockSpec((1, N_HEADS, tb, MOBA_BLOCK, HEAD_DIM), by_block),
            pl.BlockSpec((1, N_HEADS, tb, HEAD_DIM + VT_ONES_ROWS, MOBA_BLOCK), by_block),
            pl.BlockSpec((1, tb, 1, D_MODEL), lambda b, s: (b, s, 0, 0)),
        ],
        out_shape=[
            jax.ShapeDtypeStruct((bsz, n_pages, N_HEADS, PAGE_SIZE, HEAD_DIM), F32),
            jax.ShapeDtypeStruct((bsz, n_pages, N_HEADS, PAGE_SIZE, HEAD_DIM), F32),
            jax.ShapeDtypeStruct((bsz, N_HEADS, n_blk, HEAD_DIM, MOBA_BLOCK), F32),
            jax.ShapeDtypeStruct((bsz, N_HEADS, n_blk, MOBA_BLOCK, HEAD_DIM), BF16),
            jax.ShapeDtypeStruct((bsz, N_HEADS, n_blk, HEAD_DIM + VT_ONES_ROWS, MOBA_BLOCK), BF16),
            jax.ShapeDtypeStruct((bsz, n_blk, 1, D_MODEL), F32),
        ],
        compiler_params=_params("arbitrary", "arbitrary"),
        name="qkv_prompt",
    )(x, g, w_qkv, cos, slo, shi)


def _qkv_sample_kernel(x_ref, g_ref, w_ref, cos_ref, slo_ref, shi_ref, q_ref, k_ref, v_ref):
    h = _rms(x_ref[...], g_ref[...]).astype(BF16)
    qkv = jnp.dot(h, w_ref[...], preferred_element_type=F32)
    cos, slo, shi = cos_ref[...], slo_ref[...], shi_ref[...]
    for hd in range(N_HEADS):
        lo = hd * HEAD_DIM
        q_ref[:, lo:lo + HEAD_DIM] = _rope(qkv[:, lo:lo + HEAD_DIM], cos, slo, shi)
        k_ref[:, lo:lo + HEAD_DIM] = _rope(
            qkv[:, D_MODEL + lo:D_MODEL + lo + HEAD_DIM], cos, slo, shi)
    v_ref[...] = qkv[:, 2 * D_MODEL:]


def _qkv_sample(x, g, w_qkv, cos, slo, shi):
    n = x.shape[0]
    out = jax.ShapeDtypeStruct((n, D_MODEL), F32)
    return pl.pallas_call(
        _qkv_sample_kernel,
        grid=(1,),
        in_specs=[
            _full((n, D_MODEL)),
            _full((1, D_MODEL)),
            _full((D_MODEL, 3 * D_MODEL)),
            _full((n, HEAD_DIM)), _full((n, HEAD_DIM)), _full((n, HEAD_DIM)),
        ],
        out_specs=[_full((n, D_MODEL))] * 3,
        out_shape=[out, out, out],
        compiler_params=_params("arbitrary"),
        name="qkv_sample",
    )(x, g, w_qkv, cos, slo, shi)


def _attn_prompt_kernel(*refs, n_pages):
    qt_ref, k_ref, vt_ref, km_ref = refs[1:5]
    page_refs = refs[5:5 + n_pages]
    (o_ref, kmean_ref,
     bias_ref, qs_ref, m_ref, mb_ref, alpha_ref, p_ref, s_ref, acc_ref) = refs[5 + n_pages:]
    for j in range(n_pages // PAGES_PER_BLOCK):
        _mean_block(page_refs, kmean_ref, j, jnp.zeros((N_HEADS, HEAD_DIM), F32))
    qb = pl.program_id(2)
    n_heads = qt_ref.shape[1]
    n_blk = km_ref.shape[2]
    tq = qt_ref.shape[4]
    heads = range(n_heads)
    blk = lax.broadcasted_iota(jnp.int32, (n_blk, tq), 0)
    past = blk < qb
    key = lax.broadcasted_iota(jnp.int32, (tq, tq), 0)
    qry = lax.broadcasted_iota(jnp.int32, (tq, tq), 1)
    causal = jnp.where(key <= qry, 0.0, NEG_INF)

    def score_stage(j, slot):
        scores = [jnp.dot(k_ref[0, h, j], qs_ref[h], preferred_element_type=F32)
                  for h in heads]
        for h in heads:
            s = scores[h] + bias_ref[h, pl.ds(j, 1), :]
            s_ref[slot, h] = s
            mb_ref[slot, h] = jnp.max(s, axis=0, keepdims=True)

    def value_stage(j, slot):
        pvs = [jnp.dot(vt_ref[0, h, j], p_ref[slot, h], preferred_element_type=F32)
               for h in heads]
        for h in heads:
            acc_ref[h] = alpha_ref[slot, h] * acc_ref[h] + pvs[h]

    gates = [lax.dot_general(km_ref[0, h], qt_ref[0, h, 0], (((1,), (0,)), ((), ())),
                             precision=lax.Precision.HIGHEST,
                             preferred_element_type=F32) for h in heads]
    for h in heads:
        qs_ref[h] = (qt_ref[0, h, 0] * (ATTN_SCALE * LOG2_E)).astype(BF16)
    own = [jnp.dot(k_ref[0, h, qb], qs_ref[h], preferred_element_type=F32) for h in heads]
    for h in heads:
        gate = jnp.where(past, gates[h], NEG_INF)
        rank = jnp.zeros((n_blk, tq), F32)
        for jp in range(n_blk):
            row = gate[jp:jp + 1, :]
            tie = jnp.where(blk > jp, 1.0, 0.0)
            rank = rank + jnp.where(row > gate, 1.0, jnp.where(row == gate, tie, 0.0))
        bias_ref[h] = jnp.where(rank < MOBA_TOPK, jnp.where(past, 0.0, NEG_INF), NEG_INF)
    for h in heads:
        s = own[h] + causal
        m = jnp.max(s, axis=0, keepdims=True)
        m_ref[h] = m
        alpha_ref[1, h] = jnp.ones((1, tq), F32)
        p_ref[1, h] = jnp.exp2(s - m).astype(BF16)
        acc_ref[h] = jnp.zeros(acc_ref.shape[1:], F32)
    score_stage(0, 0)

    def trip(j, cur):
        nxt = 1 - cur
        score_stage(jnp.minimum(j + 1, n_blk - 1), nxt)
        for h in heads:
            m_old = m_ref[h]
            m_new = jnp.maximum(m_old, mb_ref[cur, h])
            m_ref[h] = m_new
            alpha_ref[cur, h] = jnp.exp2(m_old - m_new)
            p_ref[cur, h] = jnp.exp2(s_ref[cur, h] - m_new).astype(BF16)
        value_stage(jnp.where(j == 0, qb, j - 1), nxt)

    def body(i, carry):
        trip(2 * i, 0)

        @pl.when(2 * i + 1 < qb)
        def _():
            trip(2 * i + 1, 1)

        return carry

    lax.fori_loop(0, (qb + 1) // 2, body, 0)
    value_stage(jnp.where(qb == 0, qb, qb - 1), lax.rem(qb + 1, 2))
    for h in heads:
        o = acc_ref[h, :HEAD_DIM] * (1.0 / acc_ref[h, HEAD_DIM:HEAD_DIM + 1])
        o_ref[0, :, h * HEAD_DIM:(h + 1) * HEAD_DIM] = o.T.astype(BF16)


def _attn_prompt(qt, kb, vt, km, pages):
    bsz, _, n_blk, _, tq = qt.shape
    vt_rows = vt.shape[3]
    hps = ATTN_HEADS_PER_STEP
    n_hg = N_HEADS // hps
    step_of = lambda b, h, i: (b * n_hg + h) * n_blk + i
    return pl.pallas_call(
        functools.partial(_attn_prompt_kernel, n_pages=pages.per_step),
        grid_spec=pltpu.PrefetchScalarGridSpec(
            num_scalar_prefetch=1,
            grid=(bsz, n_hg, n_blk),
            in_specs=[
                pl.BlockSpec((1, hps, 1, HEAD_DIM, tq), lambda b, h, i, pt: (b, h, i, 0, 0)),
                pl.BlockSpec((1, hps, n_blk, tq, HEAD_DIM), lambda b, h, i, pt: (b, h, 0, 0, 0)),
                pl.BlockSpec((1, hps, n_blk, vt_rows, tq), lambda b, h, i, pt: (b, h, 0, 0, 0)),
                pl.BlockSpec((1, hps, n_blk, HEAD_DIM), lambda b, h, i, pt: (b, h, 0, 0)),
            ] + _page_specs(pages.layer, pages.first, pages.per_step, step_of),
            out_specs=[
                pl.BlockSpec((1, tq, hps * HEAD_DIM), lambda b, h, i, pt: (b, i, h)),
                _kmean_spec(pages.per_step, step_of),
            ],
            scratch_shapes=[
                pltpu.VMEM((hps, n_blk, tq), F32),
                pltpu.VMEM((hps, HEAD_DIM, tq), BF16),
                pltpu.VMEM((hps, 1, tq), F32),
                pltpu.VMEM((2, hps, 1, tq), F32),
                pltpu.VMEM((2, hps, 1, tq), F32),
                pltpu.VMEM((2, hps, tq, tq), BF16),
                pltpu.VMEM((2, hps, tq, tq), F32),
                pltpu.VMEM((hps, vt_rows, tq), F32),
            ],
        ),
        out_shape=[
            jax.ShapeDtypeStruct((bsz, n_blk * tq, D_MODEL), BF16),
            _kmean_shape(bsz * n_hg * n_blk, pages.per_step),
        ],
        compiler_params=_params("arbitrary", "arbitrary", "arbitrary"),
        name="attn_prompt",
    )(pages.table, qt, kb, vt, km, *([pages.cache] * pages.per_step))


def _select_sample_kernel(q_ref, km_ref, sel_ref):
    n_blk = km_ref.shape[1]
    prod = km_ref[0] * q_ref[0]
    gate = jnp.concatenate(
        [jnp.sum(prod[:, h * HEAD_DIM:(h + 1) * HEAD_DIM], axis=1, keepdims=True)
         for h in range(N_HEADS)], axis=1)
    blk = lax.broadcasted_iota(jnp.int32, gate.shape, 0)
    picks = []
    for _ in range(MOBA_TOPK):
        best = jnp.max(gate, axis=0, keepdims=True)
        idx = jnp.min(jnp.where(gate == best, blk, n_blk), axis=0, keepdims=True)
        picks.append(idx)
        gate = jnp.where(blk == idx, NEG_INF, gate)
    sel_ref[0] = jnp.concatenate(picks, axis=0)


def _select_sample(q, kmean):
    n_seq, n_blk = kmean.shape[0], kmean.shape[1]
    return pl.pallas_call(
        _select_sample_kernel,
        grid=(n_seq,),
        in_specs=[
            pl.BlockSpec((1, 1, D_MODEL), lambda b: (b, 0, 0)),
            pl.BlockSpec((1, n_blk, D_MODEL), lambda b: (b, 0, 0)),
        ],
        out_specs=pl.BlockSpec((1, MOBA_TOPK, N_HEADS), lambda b: (b, 0, 0)),
        out_shape=jax.ShapeDtypeStruct((n_seq, MOBA_TOPK, N_HEADS), jnp.int32),
        compiler_params=_params("arbitrary"),
        name="select_sample",
    )(q.reshape(n_seq, 1, D_MODEL), kmean.reshape(n_seq, n_blk, D_MODEL))


def _attn_sample_kernel(pt_ref, sel_ref, q_ref, kn_ref, vn_ref, *refs):
    del pt_ref, sel_ref
    n_sel_pages = (len(refs) - 1) // 2
    k_refs, v_refs, o_ref = refs[:n_sel_pages], refs[n_sel_pages:-1], refs[-1]
    q = q_ref[0, 0]
    s_new = jnp.sum(q * kn_ref[0, 0], axis=1, keepdims=True) * ATTN_SCALE
    scores = [jnp.sum(k_ref[0, 0, 0] * q, axis=1, keepdims=True) * ATTN_SCALE
              for k_ref in k_refs]
    m = s_new
    for s in scores:
        m = jnp.maximum(m, jnp.max(s, axis=0, keepdims=True))
    p_new = jnp.exp(s_new - m)
    l = p_new
    acc = p_new * vn_ref[0, 0]
    for s, v_ref in zip(scores, v_refs):
        p = jnp.exp(s - m)
        l = l + jnp.sum(p, axis=0, keepdims=True)
        acc = acc + jnp.sum(p * v_ref[0, 0, 0], axis=0, keepdims=True)
    o_ref[0, 0] = (acc * (1.0 / l)).astype(o_ref.dtype)


def _attn_sample(q, kn, vn, cache_k, cache_v, page_table, sel, layer):
    n_seq = q.shape[0]
    page_block = (1, 1, 1, PAGE_SIZE, HEAD_DIM)

    def page_spec(n, r):
        def index_map(b, h, pt, sl):
            blk = sl[(b * MOBA_TOPK + n) * N_HEADS + h]
            return (layer, pt[b, blk * PAGES_PER_BLOCK + r], h, 0, 0)
        return pl.BlockSpec(page_block, index_map)

    row = pl.BlockSpec((1, 1, 1, HEAD_DIM), lambda b, h, pt, sl: (b, h, 0, 0))
    pages = [page_spec(n, r) for n in range(MOBA_TOPK) for r in range(PAGES_PER_BLOCK)]
    return pl.pallas_call(
        _attn_sample_kernel,
        grid_spec=pltpu.PrefetchScalarGridSpec(
            num_scalar_prefetch=2,
            grid=(n_seq, N_HEADS),
            in_specs=[row, row, row] + pages + pages,
            out_specs=row,
        ),
        out_shape=jax.ShapeDtypeStruct((n_seq, N_HEADS, 1, HEAD_DIM), BF16),
        compiler_params=_params("arbitrary", "arbitrary"),
        name="attn_sample",
    )(page_table, sel, q, kn, vn, *([cache_k] * len(pages)), *([cache_v] * len(pages)))


class _Pages(NamedTuple):
    cache: jax.Array
    table: jax.Array
    layer: int
    first: int
    per_step: int


class _Plan(NamedTuple):
    tm_mix: int
    tm_post: int
    tm_qkv: int
    quarter: int

    def pages(self, cache, table, layer, idx, n_steps):
        per_step = self.quarter // n_steps
        assert per_step * n_steps == self.quarter and per_step % PAGES_PER_BLOCK == 0
        return _Pages(cache, table, layer, idx * self.quarter, per_step)


def _plan(n_cache_pages):
    assert n_cache_pages % 4 == 0
    return _Plan(tm_mix=512, tm_post=512, tm_qkv=512, quarter=n_cache_pages // 4)


def kernel(x_prompt, x_sample, state_conv, cache_k, cache_v, page_table,
           norm_mix, norm_ffn, norm_final, w_conv_in, w_conv, w_conv_out,
           w_qkv, w_o, w_gate_up, w_down):
    bsz, seq, _ = x_prompt.shape
    n_seq, n_pages = page_table.shape
    n_rows = bsz * seq
    plan = _plan(n_seq * n_pages)
    table = page_table.reshape(-1)
    row = lambda v: v.reshape(1, D_MODEL)
    w_in_b = w_conv_in[0].astype(BF16)
    w_out_b = w_conv_out[0].astype(BF16)
    w_qkv_b = w_qkv[0].astype(BF16)
    w_o_b = w_o[0].astype(BF16)
    w_gu_b = w_gate_up.astype(BF16)
    w_down_b = w_down.astype(BF16)
    g_final = row(norm_final)

    a_p, conv_p, kmean_0 = _mix_prompt(
        x_prompt, row(norm_mix[0]), w_in_b, w_conv[0], plan.tm_mix,
        plan.pages(cache_k, table, 0, 0, n_rows // plan.tm_mix))
    y_p, kmean_1 = _post(
        x_prompt.reshape(n_rows, D_MODEL), a_p.reshape(n_rows, D_MODEL),
        w_out_b, row(norm_ffn[0]), w_gu_b[0], w_down_b[0], g_final, plan.tm_post, False,
        plan.pages(cache_k, table, 0, 1, n_rows // plan.tm_post))

    xs = x_sample.reshape(n_seq, D_MODEL)
    a_s, conv_s = _mix_sample(xs, state_conv[0].transpose(1, 0, 2), row(norm_mix[0]),
                              w_in_b, w_conv[0])
    y_s = _post(xs, a_s, w_out_b, row(norm_ffn[0]), w_gu_b[0], w_down_b[0], g_final,
                n_seq, False)

    cos, slo, shi = _rope_tables(jnp.arange(seq))
    k_pages, v_pages, qt, kb, vt, km = _qkv_prompt(
        y_p.reshape(bsz, seq, D_MODEL), row(norm_mix[1]), w_qkv_b, cos, slo, shi, plan.tm_qkv)
    n_blk = seq // MOBA_BLOCK
    km_h = km.reshape(bsz, n_blk, N_HEADS, HEAD_DIM).transpose(0, 2, 1, 3)
    attn_p, kmean_2 = _attn_prompt(
        qt, kb, vt, km_h,
        plan.pages(cache_k, table, 0, 2, bsz * (N_HEADS // ATTN_HEADS_PER_STEP) * n_blk))
    y_p, kmean_3 = _post(
        y_p, attn_p.reshape(n_rows, D_MODEL), w_o_b, row(norm_ffn[1]),
        w_gu_b[1], w_down_b[1], g_final, plan.tm_post, True,
        plan.pages(cache_k, table, 0, 3, n_rows // plan.tm_post))

    cos_s, slo_s, shi_s = _rope_tables(PAST_LEN + jnp.arange(1))
    bc = lambda t: jnp.broadcast_to(t, (n_seq, HEAD_DIM))
    q_s, k_s, v_s = _qkv_sample(y_s, row(norm_mix[1]), w_qkv_b, bc(cos_s), bc(slo_s), bc(shi_s))
    kmean_s = jnp.concatenate(
        [part.reshape(-1, N_HEADS, HEAD_DIM) for part in (kmean_0, kmean_1, kmean_2, kmean_3)],
        axis=0).reshape(n_seq, n_pages // PAGES_PER_BLOCK, N_HEADS, HEAD_DIM)
    sel = _select_sample(q_s, kmean_s)
    heads = lambda t: t.reshape(n_seq, N_HEADS, 1, HEAD_DIM)
    attn_s = _attn_sample(heads(q_s), heads(k_s), heads(v_s), cache_k, cache_v,
                          page_table, sel.reshape(-1), 0)
    y_s = _post(y_s, attn_s.reshape(n_seq, D_MODEL), w_o_b, row(norm_ffn[1]),
                w_gu_b[1], w_down_b[1], g_final, n_seq, True)

    return (y_p.reshape(bsz, seq, D_MODEL),
            y_s.reshape(n_seq, 1, D_MODEL),
            conv_p[None],
            conv_s.transpose(1, 0, 2)[None],
            k_pages[None],
            v_pages[None],
            heads(k_s)[None],
            heads(v_s)[None])
```

```python
import functools
from typing import NamedTuple

import jax
import jax.numpy as jnp
from jax import lax
from jax.experimental import pallas as pl
from jax.experimental.pallas import tpu as pltpu

D_MODEL = 1024
N_HEADS = 8
HEAD_DIM = D_MODEL // N_HEADS
ROT_DIM = HEAD_DIM // 4
ROPE_THETA = 500000.0
MOBA_BLOCK = 256
MOBA_TOPK = 3
PAGE_SIZE = 128
PAGES_PER_BLOCK = MOBA_BLOCK // PAGE_SIZE
PAST_LEN = 16384
RMS_EPS = 1e-6
ATTN_SCALE = HEAD_DIM ** -0.5
LOG2_E = 1.4426950408889634

VMEM_LIMIT_BYTES = 56 * 1024 * 1024
CONV_PAD = 8
VT_ONES_ROWS = 16
POST_GROUP_ROWS = 256
SELECT_SEQS_PER_STEP = 8
ATTN_HEADS_PER_STEP = 4

F32 = jnp.float32
BF16 = jnp.bfloat16
NEG_INF = float("-inf")


def _rms(x, g):
    return x * lax.rsqrt(jnp.mean(x * x, axis=-1, keepdims=True) + RMS_EPS) * g


def _rope(x, cos, sin_lo, sin_hi):
    half = ROT_DIM // 2
    return (x * cos
            + pltpu.roll(x, HEAD_DIM - half, axis=1) * sin_lo
            + pltpu.roll(x, half, axis=1) * sin_hi)


def _rope_tables(pos):
    half = ROT_DIM // 2
    inv = jnp.power(F32(ROPE_THETA), -jnp.arange(half, dtype=F32) * 2.0 / ROT_DIM)
    ang = pos.astype(F32)[:, None] * inv[None, :]
    cos, sin = jnp.cos(ang), jnp.sin(ang)
    t = pos.shape[0]
    rest = HEAD_DIM - ROT_DIM
    cos_f = jnp.concatenate([cos, cos, jnp.ones((t, rest), F32)], axis=1)
    sin_lo = jnp.concatenate([-sin, jnp.zeros((t, half + rest), F32)], axis=1)
    sin_hi = jnp.concatenate([jnp.zeros((t, half), F32), sin, jnp.zeros((t, rest), F32)], axis=1)
    return cos_f, sin_lo, sin_hi


def _params(*semantics):
    return pltpu.CompilerParams(dimension_semantics=semantics,
                                vmem_limit_bytes=VMEM_LIMIT_BYTES)


def _full(shape):
    return pl.BlockSpec(shape, lambda *_: (0,) * len(shape))


def _page_specs(layer, first_page, pages_per_step, step_of):
    def spec(r):
        def index_map(*args):
            page = first_page + step_of(*args[:-1]) * pages_per_step + r
            return (layer, args[-1][page], 0, 0, 0)
        return pl.BlockSpec((1, 1, N_HEADS, PAGE_SIZE, HEAD_DIM), index_map)
    return [spec(r) for r in range(pages_per_step)]


def _kmean_spec(pages_per_step, step_of):
    blocks = pages_per_step // PAGES_PER_BLOCK
    return pl.BlockSpec((1, blocks, N_HEADS, HEAD_DIM),
                        lambda *args: (step_of(*args[:-1]), 0, 0, 0))


def _kmean_shape(n_steps, pages_per_step):
    return jax.ShapeDtypeStruct(
        (n_steps, pages_per_step // PAGES_PER_BLOCK, N_HEADS, HEAD_DIM), F32)


def _mean_block(page_refs, kmean_ref, j, zero):
    tot = zero + jnp.sum(page_refs[j * PAGES_PER_BLOCK][0, 0], axis=1)
    for r in range(1, PAGES_PER_BLOCK):
        tot = tot + jnp.sum(page_refs[j * PAGES_PER_BLOCK + r][0, 0], axis=1)
    mean = tot / MOBA_BLOCK
    kmean_ref[0, j] = mean
    return mean


def _zero_after(x):
    bits = lax.bitcast_convert_type(x, jnp.uint32)
    return lax.bitcast_convert_type((bits >> 31) >> 1, F32)


def _dot_with_page_means(x, x_ref, w_ref, n_chunks, page_refs, kmean_ref):
    n = w_ref.shape[1] // n_chunks
    n_blocks = len(page_refs) // PAGES_PER_BLOCK
    per_gap = -(-n_blocks // max(n_chunks - 1, 1))
    x_ref[...] = x
    start = _zero_after(x[0:N_HEADS, 0:HEAD_DIM].astype(F32))
    outs, j = [], 0
    for c in range(n_chunks):
        outs.append(jnp.dot(x_ref[...], w_ref[:, c * n:(c + 1) * n],
                            preferred_element_type=F32))
        tie = None
        while j < min(n_blocks, (c + 1) * per_gap):
            mean = _mean_block(page_refs, kmean_ref, j, start)
            tie = mean if tie is None else tie + mean
            j += 1
        if tie is not None:
            zero = _zero_after(jnp.concatenate([tie, tie], axis=0)).astype(x_ref.dtype)
            x_ref[0:2 * N_HEADS, 0:HEAD_DIM] = x_ref[0:2 * N_HEADS, 0:HEAD_DIM] + zero
    return outs


def _mix_prompt_kernel(*refs, n_pages):
    x_ref, g_ref, win_ref, wc_ref = refs[1:5]
    page_refs = refs[5:5 + n_pages]
    a_ref, st_ref, kmean_ref, uext_ref, h_ref = refs[5 + n_pages:]
    s = pl.program_id(1)
    tm = x_ref.shape[1]

    @pl.when(s == 0)
    def _():
        uext_ref[0:CONV_PAD, :] = jnp.zeros((CONV_PAD, D_MODEL), F32)

    h = _rms(x_ref[0], g_ref[...]).astype(BF16)
    parts = _dot_with_page_means(h, h_ref, win_ref, 6, page_refs, kmean_ref)
    b, c, xin = [jnp.concatenate(parts[2 * i:2 * i + 2], axis=1) for i in range(3)]
    u = c * xin
    uext_ref[CONV_PAD:CONV_PAD + tm, :] = u
    um1 = uext_ref[CONV_PAD - 1:CONV_PAD - 1 + tm, :]
    um2 = uext_ref[CONV_PAD - 2:CONV_PAD - 2 + tm, :]
    conv = wc_ref[0:1, :] * um2 + wc_ref[1:2, :] * um1 + wc_ref[2:3, :] * u
    a_ref[0] = (b * conv).astype(BF16)
    tail = u[tm - 2:tm, :]
    uext_ref[CONV_PAD - 2:CONV_PAD, :] = tail

    @pl.when(s == pl.num_programs(1) - 1)
    def _():
        st_ref[0] = tail


def _mix_prompt(x, g, w_in, w_conv, tm, pages):
    bsz, seq, _ = x.shape
    n_s = seq // tm
    step_of = lambda b, s: b * n_s + s
    return pl.pallas_call(
        functools.partial(_mix_prompt_kernel, n_pages=pages.per_step),
        grid_spec=pltpu.PrefetchScalarGridSpec(
            num_scalar_prefetch=1,
            grid=(bsz, n_s),
            in_specs=[
                pl.BlockSpec((1, tm, D_MODEL), lambda b, s, pt: (b, s, 0)),
                _full((1, D_MODEL)),
                _full((D_MODEL, 3 * D_MODEL)),
                _full((3, D_MODEL)),
            ] + _page_specs(pages.layer, pages.first, pages.per_step, step_of),
            out_specs=[
                pl.BlockSpec((1, tm, D_MODEL), lambda b, s, pt: (b, s, 0)),
                pl.BlockSpec((1, 2, D_MODEL), lambda b, s, pt: (b, 0, 0)),
                _kmean_spec(pages.per_step, step_of),
            ],
            scratch_shapes=[pltpu.VMEM((CONV_PAD + tm, D_MODEL), F32),
                            pltpu.VMEM((tm, D_MODEL), BF16)],
        ),
        out_shape=[
            jax.ShapeDtypeStruct((bsz, seq, D_MODEL), BF16),
            jax.ShapeDtypeStruct((bsz, 2, D_MODEL), F32),
            _kmean_shape(bsz * n_s, pages.per_step),
        ],
        compiler_params=_params("arbitrary", "arbitrary"),
        name="mix_prompt",
    )(pages.table, x, g, w_in, w_conv, *([pages.cache] * pages.per_step))


def _mix_sample_kernel(x_ref, st_ref, g_ref, win_ref, wc_ref, a_ref, nst_ref):
    h = _rms(x_ref[...], g_ref[...]).astype(BF16)
    bcx = jnp.dot(h, win_ref[...], preferred_element_type=F32)
    b = bcx[:, :D_MODEL]
    u = bcx[:, D_MODEL:2 * D_MODEL] * bcx[:, 2 * D_MODEL:]
    um2 = st_ref[0]
    um1 = st_ref[1]
    conv = wc_ref[0:1, :] * um2 + wc_ref[1:2, :] * um1 + wc_ref[2:3, :] * u
    a_ref[...] = (b * conv).astype(BF16)
    nst_ref[0] = um1
    nst_ref[1] = u


def _mix_sample(x, state_t, g, w_in, w_conv):
    n = x.shape[0]
    return pl.pallas_call(
        _mix_sample_kernel,
        grid=(1,),
        in_specs=[
            _full((n, D_MODEL)),
            _full((2, n, D_MODEL)),
            _full((1, D_MODEL)),
            _full((D_MODEL, 3 * D_MODEL)),
            _full((3, D_MODEL)),
        ],
        out_specs=[_full((n, D_MODEL)), _full((2, n, D_MODEL))],
        out_shape=[
            jax.ShapeDtypeStruct((n, D_MODEL), BF16),
            jax.ShapeDtypeStruct((2, n, D_MODEL), F32),
        ],
        compiler_params=_params("arbitrary"),
        name="mix_sample",
    )(x, state_t, g, w_in, w_conv)


def _post_kernel(*refs, final_norm, n_pages):
    if n_pages:
        refs = refs[1:]
    res_ref, a_ref, wpre_ref, g_ref, wgu_ref, wd_ref, gf_ref = refs[:7]
    y_ref = refs[7 + n_pages]
    kmean_ref = refs[8 + n_pages] if n_pages else None
    h_ref = refs[-1]
    tm = res_ref.shape[0]
    n_groups = 2 if tm >= 2 * POST_GROUP_ROWS else 1
    rows = [slice(r * tm // n_groups, (r + 1) * tm // n_groups) for r in range(n_groups)]
    y1 = [res_ref[r, :] + jnp.dot(a_ref[r, :], wpre_ref[...], preferred_element_type=F32)
          for r in rows]
    h = [_rms(y, g_ref[...]).astype(BF16) for y in y1]
    gu = []
    for i, r in enumerate(rows):
        pages = refs[7:7 + n_pages] if i == 0 else ()
        gu.append(_dot_with_page_means(h[i], h_ref.at[r, :], wgu_ref, 2, pages, kmean_ref))
    act = [(gate * (1.0 / (1.0 + jnp.exp(-gate))) * up).astype(BF16) for gate, up in gu]
    for i, r in enumerate(rows):
        y2 = y1[i] + jnp.dot(act[i], wd_ref[...], preferred_element_type=F32)
        if final_norm:
            y2 = _rms(y2, gf_ref[...])
        y_ref[r, :] = y2


def _post(res, a, w_pre, g_ffn, w_gu, w_down, g_final, tm, final_norm, pages=None):
    m = res.shape[0]
    d_ff = w_down.shape[0]
    tile = pl.BlockSpec((tm, D_MODEL), lambda i, *_: (i, 0))
    in_specs = [
        tile, tile,
        _full((D_MODEL, D_MODEL)),
        _full((1, D_MODEL)),
        _full((D_MODEL, 2 * d_ff)),
        _full((d_ff, D_MODEL)),
        _full((1, D_MODEL)),
    ]
    out_specs = [tile]
    out_shape = [jax.ShapeDtypeStruct((m, D_MODEL), F32)]
    operands = [res, a, w_pre, g_ffn, w_gu, w_down, g_final]
    n_pages = 0
    if pages is not None:
        n_pages = pages.per_step
        step_of = lambda i: i
        in_specs += _page_specs(pages.layer, pages.first, n_pages, step_of)
        out_specs.append(_kmean_spec(n_pages, step_of))
        out_shape.append(_kmean_shape(m // tm, n_pages))
        operands = [pages.table] + operands + [pages.cache] * n_pages
    out = pl.pallas_call(
        functools.partial(_post_kernel, final_norm=final_norm, n_pages=n_pages),
        grid_spec=pltpu.PrefetchScalarGridSpec(
            num_scalar_prefetch=1 if n_pages else 0,
            grid=(m // tm,),
            in_specs=in_specs,
            out_specs=out_specs,
            scratch_shapes=[pltpu.VMEM((tm, D_MODEL), BF16)],
        ),
        out_shape=out_shape,
        compiler_params=_params("arbitrary"),
        name="post_final" if final_norm else "post",
    )(*operands)
    return out if n_pages else out[0]


def _qkv_prompt_kernel(x_ref, g_ref, w_ref, cos_ref, slo_ref, shi_ref,
                       kp_ref, vp_ref, qt_ref, kb_ref, vt_ref, km_ref):
    tm = x_ref.shape[1]
    h = _rms(x_ref[0], g_ref[...]).astype(BF16)
    qkv = jnp.dot(h, w_ref[...], preferred_element_type=F32)
    cos, slo, shi = cos_ref[...], slo_ref[...], shi_ref[...]
    for hd in range(N_HEADS):
        lo = hd * HEAD_DIM
        q = _rope(qkv[:, lo:lo + HEAD_DIM], cos, slo, shi)
        k = _rope(qkv[:, D_MODEL + lo:D_MODEL + lo + HEAD_DIM], cos, slo, shi)
        v = qkv[:, 2 * D_MODEL + lo:2 * D_MODEL + lo + HEAD_DIM]
        for p in range(tm // PAGE_SIZE):
            kp_ref[0, p, hd] = k[p * PAGE_SIZE:(p + 1) * PAGE_SIZE]
            vp_ref[0, p, hd] = v[p * PAGE_SIZE:(p + 1) * PAGE_SIZE]
        for j in range(tm // MOBA_BLOCK):
            rows = slice(j * MOBA_BLOCK, (j + 1) * MOBA_BLOCK)
            qt_ref[0, hd, j] = q[rows].T
            kb_ref[0, hd, j] = k[rows].astype(BF16)
            vt_ref[0, hd, j, :HEAD_DIM] = v[rows].T.astype(BF16)
            vt_ref[0, hd, j, HEAD_DIM:] = jnp.ones((VT_ONES_ROWS, MOBA_BLOCK), BF16)
            km_ref[0, j, :, lo:lo + HEAD_DIM] = jnp.mean(k[rows], axis=0, keepdims=True)


def _qkv_prompt(x, g, w_qkv, cos, slo, shi, tm):
    bsz, seq, _ = x.shape
    n_pages, n_blk = seq // PAGE_SIZE, seq // MOBA_BLOCK
    tp, tb = tm // PAGE_SIZE, tm // MOBA_BLOCK
    tab = pl.BlockSpec((tm, HEAD_DIM), lambda b, s: (s, 0))
    by_page = lambda b, s: (b, s, 0, 0, 0)
    by_block = lambda b, s: (b, 0, s, 0, 0)
    return pl.pallas_call(
        _qkv_prompt_kernel,
        grid=(bsz, seq // tm),
        in_specs=[
            pl.BlockSpec((1, tm, D_MODEL), lambda b, s: (b, s, 0)),
            _full((1, D_MODEL)),
            _full((D_MODEL, 3 * D_MODEL)),
            tab, tab, tab,
        ],
        out_specs=[
            pl.BlockSpec((1, tp, N_HEADS, PAGE_SIZE, HEAD_DIM), by_page),
            pl.BlockSpec((1, tp, N_HEADS, PAGE_SIZE, HEAD_DIM), by_page),
            pl.BlockSpec((1, N_HEADS, tb, HEAD_DIM, MOBA_BLOCK), by_block),
            pl.BlockSpec((1, N_HEADS, tb, MOBA_BLOCK, HEAD_DIM), by_block),
            pl.BlockSpec((1, N_HEADS, tb, HEAD_DIM + VT_ONES_ROWS, MOBA_BLOCK), by_block),
            pl.BlockSpec((1, tb, 1, D_MODEL), lambda b, s: (b, s, 0, 0)),
        ],
        out_shape=[
            jax.ShapeDtypeStruct((bsz, n_pages, N_HEADS, PAGE_SIZE, HEAD_DIM), F32),
            jax.ShapeDtypeStruct((bsz, n_pages, N_HEADS, PAGE_SIZE, HEAD_DIM), F32),
            jax.ShapeDtypeStruct((bsz, N_HEADS, n_blk, HEAD_DIM, MOBA_BLOCK), F32),
            jax.ShapeDtypeStruct((bsz, N_HEADS, n_blk, MOBA_BLOCK, HEAD_DIM), BF16),
            jax.ShapeDtypeStruct((bsz, N_HEADS, n_blk, HEAD_DIM + VT_ONES_ROWS, MOBA_BLOCK), BF16),
            jax.ShapeDtypeStruct((bsz, n_blk, 1, D_MODEL), F32),
        ],
        compiler_params=_params("arbitrary", "arbitrary"),
        name="qkv_prompt",
    )(x, g, w_qkv, cos, slo, shi)


def _qkv_sample_kernel(x_ref, g_ref, w_ref, cos_ref, slo_ref, shi_ref, q_ref, k_ref, v_ref):
    h = _rms(x_ref[...], g_ref[...]).astype(BF16)
    qkv = jnp.dot(h, w_ref[...], preferred_element_type=F32)
    cos, slo, shi = cos_ref[...], slo_ref[...], shi_ref[...]
    for hd in range(N_HEADS):
        lo = hd * HEAD_DIM
        q_ref[:, lo:lo + HEAD_DIM] = _rope(qkv[:, lo:lo + HEAD_DIM], cos, slo, shi)
        k_ref[:, lo:lo + HEAD_DIM] = _rope(
            qkv[:, D_MODEL + lo:D_MODEL + lo + HEAD_DIM], cos, slo, shi)
    v_ref[...] = qkv[:, 2 * D_MODEL:]


def _qkv_sample(x, g, w_qkv, cos, slo, shi):
    n = x.shape[0]
    out = jax.ShapeDtypeStruct((n, D_MODEL), F32)
    return pl.pallas_call(
        _qkv_sample_kernel,
        grid=(1,),
        in_specs=[
            _full((n, D_MODEL)),
            _full((1, D_MODEL)),
            _full((D_MODEL, 3 * D_MODEL)),
            _full((n, HEAD_DIM)), _full((n, HEAD_DIM)), _full((n, HEAD_DIM)),
        ],
        out_specs=[_full((n, D_MODEL))] * 3,
        out_shape=[out, out, out],
        compiler_params=_params("arbitrary"),
        name="qkv_sample",
    )(x, g, w_qkv, cos, slo, shi)


def _attn_prompt_kernel(*refs, n_pages):
    qt_ref, k_ref, vt_ref, km_ref = refs[1:5]
    page_refs = refs[5:5 + n_pages]
    (o_ref, kmean_ref,
     bias_ref, qs_ref, m_ref, mb_ref, alpha_ref, p_ref, s_ref, acc_ref) = refs[5 + n_pages:]
    for j in range(n_pages // PAGES_PER_BLOCK):
        _mean_block(page_refs, kmean_ref, j, jnp.zeros((N_HEADS, HEAD_DIM), F32))
    qb = pl.program_id(2)
    n_heads = qt_ref.shape[1]
    n_blk = km_ref.shape[2]
    tq = qt_ref.shape[4]
    heads = range(n_heads)
    blk = lax.broadcasted_iota(jnp.int32, (n_blk, tq), 0)
    past = blk < qb
    key = lax.broadcasted_iota(jnp.int32, (tq, tq), 0)
    qry = lax.broadcasted_iota(jnp.int32, (tq, tq), 1)
    causal = jnp.where(key <= qry, 0.0, NEG_INF)

    def score_stage(j, slot):
        scores = [jnp.dot(k_ref[0, h, j], qs_ref[h], preferred_element_type=F32)
                  for h in heads]
        for h in heads:
            s = scores[h] + bias_ref[h, pl.ds(j, 1), :]
            s_ref[slot, h] = s
            mb_ref[slot, h] = jnp.max(s, axis=0, keepdims=True)

    def value_stage(j, slot):
        pvs = [jnp.dot(vt_ref[0, h, j], p_ref[slot, h], preferred_element_type=F32)
               for h in heads]
        for h in heads:
            acc_ref[h] = alpha_ref[slot, h] * acc_ref[h] + pvs[h]

    gates = [lax.dot_general(km_ref[0, h], qt_ref[0, h, 0], (((1,), (0,)), ((), ())),
                             precision=lax.Precision.HIGHEST,
                             preferred_element_type=F32) for h in heads]
    for h in heads:
        qs_ref[h] = (qt_ref[0, h, 0] * (ATTN_SCALE * LOG2_E)).astype(BF16)
    own = [jnp.dot(k_ref[0, h, qb], qs_ref[h], preferred_element_type=F32) for h in heads]
    for h in heads:
        gate = jnp.where(past, gates[h], NEG_INF)
        rank = jnp.zeros((n_blk, tq), F32)
        for jp in range(n_blk):
            row = gate[jp:jp + 1, :]
            tie = jnp.where(blk > jp, 1.0, 0.0)
            rank = rank + jnp.where(row > gate, 1.0, jnp.where(row == gate, tie, 0.0))
        bias_ref[h] = jnp.where(rank < MOBA_TOPK, jnp.where(past, 0.0, NEG_INF), NEG_INF)
    for h in heads:
        s = own[h] + causal
        m = jnp.max(s, axis=0, keepdims=True)
        m_ref[h] = m
        alpha_ref[1, h] = jnp.ones((1, tq), F32)
        p_ref[1, h] = jnp.exp2(s - m).astype(BF16)
        acc_ref[h] = jnp.zeros(acc_ref.shape[1:], F32)
    score_stage(0, 0)

    def trip(j, cur):
        nxt = 1 - cur
        score_stage(jnp.minimum(j + 1, n_blk - 1), nxt)
        for h in heads:
            m_old = m_ref[h]
            m_new = jnp.maximum(m_old, mb_ref[cur, h])
            m_ref[h] = m_new
            alpha_ref[cur, h] = jnp.exp2(m_old - m_new)
            p_ref[cur, h] = jnp.exp2(s_ref[cur, h] - m_new).astype(BF16)
        value_stage(jnp.where(j == 0, qb, j - 1), nxt)

    def body(i, carry):
        trip(2 * i, 0)

        @pl.when(2 * i + 1 < qb)
        def _():
            trip(2 * i + 1, 1)

        return carry

    lax.fori_loop(0, (qb + 1) // 2, body, 0)
    value_stage(jnp.where(qb == 0, qb, qb - 1), lax.rem(qb + 1, 2))
    for h in heads:
        o = acc_ref[h, :HEAD_DIM] * (1.0 / acc_ref[h, HEAD_DIM:HEAD_DIM + 1])
        o_ref[0, :, h * HEAD_DIM:(h + 1) * HEAD_DIM] = o.T.astype(BF16)


def _attn_prompt(qt, kb, vt, km, pages):
    bsz, _, n_blk, _, tq = qt.shape
    vt_rows = vt.shape[3]
    hps = ATTN_HEADS_PER_STEP
    n_hg = N_HEADS // hps
    step_of = lambda b, h, i: (b * n_hg + h) * n_blk + i
    return pl.pallas_call(
        functools.partial(_attn_prompt_kernel, n_pages=pages.per_step),
        grid_spec=pltpu.PrefetchScalarGridSpec(
            num_scalar_prefetch=1,
            grid=(bsz, n_hg, n_blk),
            in_specs=[
                pl.BlockSpec((1, hps, 1, HEAD_DIM, tq), lambda b, h, i, pt: (b, h, i, 0, 0)),
                pl.BlockSpec((1, hps, n_blk, tq, HEAD_DIM), lambda b, h, i, pt: (b, h, 0, 0, 0)),
                pl.BlockSpec((1, hps, n_blk, vt_rows, tq), lambda b, h, i, pt: (b, h, 0, 0, 0)),
                pl.BlockSpec((1, hps, n_blk, HEAD_DIM), lambda b, h, i, pt: (b, h, 0, 0)),
            ] + _page_specs(pages.layer, pages.first, pages.per_step, step_of),
            out_specs=[
                pl.BlockSpec((1, tq, hps * HEAD_DIM), lambda b, h, i, pt: (b, i, h)),
                _kmean_spec(pages.per_step, step_of),
            ],
            scratch_shapes=[
                pltpu.VMEM((hps, n_blk, tq), F32),
                pltpu.VMEM((hps, HEAD_DIM, tq), BF16),
                pltpu.VMEM((hps, 1, tq), F32),
                pltpu.VMEM((2, hps, 1, tq), F32),
                pltpu.VMEM((2, hps, 1, tq), F32),
                pltpu.VMEM((2, hps, tq, tq), BF16),
                pltpu.VMEM((2, hps, tq, tq), F32),
                pltpu.VMEM((hps, vt_rows, tq), F32),
            ],
        ),
        out_shape=[
            jax.ShapeDtypeStruct((bsz, n_blk * tq, D_MODEL), BF16),
            _kmean_shape(bsz * n_hg * n_blk, pages.per_step),
        ],
        compiler_params=_params("arbitrary", "arbitrary", "arbitrary"),
        name="attn_prompt",
    )(pages.table, qt, kb, vt, km, *([pages.cache] * pages.per_step))


def _select_sample_kernel(q_ref, km_ref, sel_ref):
    n_blk = km_ref.shape[1]
    for b in range(q_ref.shape[0]):
        prod = km_ref[b] * q_ref[b]
        gate = jnp.concatenate(
            [jnp.sum(prod[:, h * HEAD_DIM:(h + 1) * HEAD_DIM], axis=1, keepdims=True)
             for h in range(N_HEADS)], axis=1)
        blk = lax.broadcasted_iota(jnp.int32, gate.shape, 0)
        picks = []
        for _ in range(MOBA_TOPK):
            best = jnp.max(gate, axis=0, keepdims=True)
            idx = jnp.min(jnp.where(gate == best, blk, n_blk), axis=0, keepdims=True)
            picks.append(idx)
            gate = jnp.where(blk == idx, NEG_INF, gate)
        sel_ref[b] = jnp.concatenate(picks, axis=0)


def _select_sample(q, kmean):
    n_seq, n_blk = kmean.shape[0], kmean.shape[1]
    per_step = SELECT_SEQS_PER_STEP if n_seq % SELECT_SEQS_PER_STEP == 0 else 1
    return pl.pallas_call(
        _select_sample_kernel,
        grid=(n_seq // per_step,),
        in_specs=[
            pl.BlockSpec((per_step, 1, D_MODEL), lambda b: (b, 0, 0)),
            pl.BlockSpec((per_step, n_blk, D_MODEL), lambda b: (b, 0, 0)),
        ],
        out_specs=pl.BlockSpec((per_step, MOBA_TOPK, N_HEADS), lambda b: (b, 0, 0)),
        out_shape=jax.ShapeDtypeStruct((n_seq, MOBA_TOPK, N_HEADS), jnp.int32),
        compiler_params=_params("arbitrary"),
        name="select_sample",
    )(q.reshape(n_seq, 1, D_MODEL), kmean.reshape(n_seq, n_blk, D_MODEL))


def _attn_sample_kernel(pt_ref, sel_ref, q_ref, kn_ref, vn_ref, *refs):
    del pt_ref, sel_ref
    n_sel_pages = (len(refs) - 1) // 2
    k_refs, v_refs, o_ref = refs[:n_sel_pages], refs[n_sel_pages:-1], refs[-1]
    q = q_ref[0, 0]
    s_new = jnp.sum(q * kn_ref[0, 0], axis=1, keepdims=True) * ATTN_SCALE
    scores = [jnp.sum(k_ref[0, 0, 0] * q, axis=1, keepdims=True) * ATTN_SCALE
              for k_ref in k_refs]
    m = s_new
    for s in scores:
        m = jnp.maximum(m, jnp.max(s, axis=0, keepdims=True))
    p_new = jnp.exp(s_new - m)
    l = p_new
    acc = p_new * vn_ref[0, 0]
    for s, v_ref in zip(scores, v_refs):
        p = jnp.exp(s - m)
        l = l + jnp.sum(p, axis=0, keepdims=True)
        acc = acc + jnp.sum(p * v_ref[0, 0, 0], axis=0, keepdims=True)
    o_ref[0, 0] = (acc * (1.0 / l)).astype(o_ref.dtype)


def _attn_sample(q, kn, vn, cache_k, cache_v, page_table, sel, layer):
    n_seq = q.shape[0]
    page_block = (1, 1, 1, PAGE_SIZE, HEAD_DIM)

    def page_spec(n, r):
        def index_map(b, h, pt, sl):
            blk = sl[(b * MOBA_TOPK + n) * N_HEADS + h]
            return (layer, pt[b, blk * PAGES_PER_BLOCK + r], h, 0, 0)
        return pl.BlockSpec(page_block, index_map)

    row = pl.BlockSpec((1, 1, 1, HEAD_DIM), lambda b, h, pt, sl: (b, h, 0, 0))
    pages = [page_spec(n, r) for n in range(MOBA_TOPK) for r in range(PAGES_PER_BLOCK)]
    return pl.pallas_call(
        _attn_sample_kernel,
        grid_spec=pltpu.PrefetchScalarGridSpec(
            num_scalar_prefetch=2,
            grid=(n_seq, N_HEADS),
            in_specs=[row, row, row] + pages + pages,
            out_specs=row,
        ),
        out_shape=jax.ShapeDtypeStruct((n_seq, N_HEADS, 1, HEAD_DIM), BF16),
        compiler_params=_params("arbitrary", "arbitrary"),
        name="attn_sample",
    )(page_table, sel, q, kn, vn, *([cache_k] * len(pages)), *([cache_v] * len(pages)))


class _Pages(NamedTuple):
    cache: jax.Array
    table: jax.Array
    layer: int
    first: int
    per_step: int


class _Plan(NamedTuple):
    tm_mix: int
    tm_post: int
    tm_qkv: int
    quarter: int

    def pages(self, cache, table, layer, idx, n_steps):
        per_step = self.quarter // n_steps
        assert per_step * n_steps == self.quarter and per_step % PAGES_PER_BLOCK == 0
        return _Pages(cache, table, layer, idx * self.quarter, per_step)


def _plan(n_cache_pages):
    assert n_cache_pages % 4 == 0
    return _Plan(tm_mix=512, tm_post=512, tm_qkv=512, quarter=n_cache_pages // 4)


def kernel(x_prompt, x_sample, state_conv, cache_k, cache_v, page_table,
           norm_mix, norm_ffn, norm_final, w_conv_in, w_conv, w_conv_out,
           w_qkv, w_o, w_gate_up, w_down):
    bsz, seq, _ = x_prompt.shape
    n_seq, n_pages = page_table.shape
    n_rows = bsz * seq
    plan = _plan(n_seq * n_pages)
    table = page_table.reshape(-1)
    row = lambda v: v.reshape(1, D_MODEL)
    w_in_b = w_conv_in[0].astype(BF16)
    w_out_b = w_conv_out[0].astype(BF16)
    w_qkv_b = w_qkv[0].astype(BF16)
    w_o_b = w_o[0].astype(BF16)
    w_gu_b = [w.astype(BF16) for w in w_gate_up]
    w_down_b = [w.astype(BF16) for w in w_down]
    g_final = row(norm_final)

    a_p, conv_p, kmean_0 = _mix_prompt(
        x_prompt, row(norm_mix[0]), w_in_b, w_conv[0], plan.tm_mix,
        plan.pages(cache_k, table, 0, 0, n_rows // plan.tm_mix))
    y_p, kmean_1 = _post(
        x_prompt.reshape(n_rows, D_MODEL), a_p.reshape(n_rows, D_MODEL),
        w_out_b, row(norm_ffn[0]), w_gu_b[0], w_down_b[0], g_final, plan.tm_post, False,
        plan.pages(cache_k, table, 0, 1, n_rows // plan.tm_post))

    xs = x_sample.reshape(n_seq, D_MODEL)
    a_s, conv_s = _mix_sample(xs, state_conv[0].transpose(1, 0, 2), row(norm_mix[0]),
                              w_in_b, w_conv[0])
    y_s = _post(xs, a_s, w_out_b, row(norm_ffn[0]), w_gu_b[0], w_down_b[0], g_final,
                n_seq, False)

    cos, slo, shi = _rope_tables(jnp.arange(seq))
    k_pages, v_pages, qt, kb, vt, km = _qkv_prompt(
        y_p.reshape(bsz, seq, D_MODEL), row(norm_mix[1]), w_qkv_b, cos, slo, shi, plan.tm_qkv)
    n_blk = seq // MOBA_BLOCK
    km_h = km.reshape(bsz, n_blk, N_HEADS, HEAD_DIM).transpose(0, 2, 1, 3)
    attn_p, kmean_2 = _attn_prompt(
        qt, kb, vt, km_h,
        plan.pages(cache_k, table, 0, 2, bsz * (N_HEADS // ATTN_HEADS_PER_STEP) * n_blk))
    y_p, kmean_3 = _post(
        y_p, attn_p.reshape(n_rows, D_MODEL), w_o_b, row(norm_ffn[1]),
        w_gu_b[1], w_down_b[1], g_final, plan.tm_post, True,
        plan.pages(cache_k, table, 0, 3, n_rows // plan.tm_post))

    cos_s, slo_s, shi_s = _rope_tables(PAST_LEN + jnp.arange(1))
    bc = lambda t: jnp.broadcast_to(t, (n_seq, HEAD_DIM))
    q_s, k_s, v_s = _qkv_sample(y_s, row(norm_mix[1]), w_qkv_b, bc(cos_s), bc(slo_s), bc(shi_s))
    kmean_s = jnp.concatenate(
        [part.reshape(-1, N_HEADS, HEAD_DIM) for part in (kmean_0, kmean_1, kmean_2, kmean_3)],
        axis=0).reshape(n_seq, n_pages // PAGES_PER_BLOCK, N_HEADS, HEAD_DIM)
    sel = _select_sample(q_s, kmean_s)
    heads = lambda t: t.reshape(n_seq, N_HEADS, 1, HEAD_DIM)
    attn_s = _attn_sample(heads(q_s), heads(k_s), heads(v_s), cache_k, cache_v,
                          page_table, sel.reshape(-1), 0)
    y_s = _post(y_s, attn_s.reshape(n_seq, D_MODEL), w_o_b, row(norm_ffn[1]),
                w_gu_b[1], w_down_b[1], g_final, n_seq, True)

    return (y_p.reshape(bsz, seq, D_MODEL),
            y_s.reshape(n_seq, 1, D_MODEL),
            conv_p[None],
            conv_s.transpose(1, 0, 2)[None],
            k_pages[None],
            v_pages[None],
            heads(k_s)[None],
            heads(v_s)[None])
```

```python
import functools
from typing import NamedTuple

import jax
import jax.numpy as jnp
from jax import lax
from jax.experimental import pallas as pl
from jax.experimental.pallas import tpu as pltpu

D_MODEL = 1024
N_HEADS = 8
HEAD_DIM = D_MODEL // N_HEADS
ROT_DIM = HEAD_DIM // 4
ROPE_THETA = 500000.0
MOBA_BLOCK = 256
MOBA_TOPK = 3
PAGE_SIZE = 128
PAGES_PER_BLOCK = MOBA_BLOCK // PAGE_SIZE
PAST_LEN = 16384
RMS_EPS = 1e-6
ATTN_SCALE = HEAD_DIM ** -0.5
LOG2_E = 1.4426950408889634

VMEM_LIMIT_BYTES = 56 * 1024 * 1024
CONV_PAD = 8
VT_ONES_ROWS = 16
POST_GROUP_ROWS = 256
SELECT_SEQS_PER_STEP = 8
ATTN_HEADS_PER_STEP = 4

F32 = jnp.float32
BF16 = jnp.bfloat16
NEG_INF = float("-inf")


def _rms(x, g):
    return x * lax.rsqrt(jnp.mean(x * x, axis=-1, keepdims=True) + RMS_EPS) * g


def _rope(x, cos, sin_lo, sin_hi):
    half = ROT_DIM // 2
    return (x * cos
            + pltpu.roll(x, HEAD_DIM - half, axis=1) * sin_lo
            + pltpu.roll(x, half, axis=1) * sin_hi)


def _rope_tables(pos):
    half = ROT_DIM // 2
    inv = jnp.power(F32(ROPE_THETA), -jnp.arange(half, dtype=F32) * 2.0 / ROT_DIM)
    ang = pos.astype(F32)[:, None] * inv[None, :]
    cos, sin = jnp.cos(ang), jnp.sin(ang)
    t = pos.shape[0]
    rest = HEAD_DIM - ROT_DIM
    cos_f = jnp.concatenate([cos, cos, jnp.ones((t, rest), F32)], axis=1)
    sin_lo = jnp.concatenate([-sin, jnp.zeros((t, half + rest), F32)], axis=1)
    sin_hi = jnp.concatenate([jnp.zeros((t, half), F32), sin, jnp.zeros((t, rest), F32)], axis=1)
    return cos_f, sin_lo, sin_hi


def _split_bf16(x):
    hi = x.astype(BF16)
    return hi, (x - hi.astype(F32)).astype(BF16)


def _dot_3pass(a, b):
    a_hi, a_lo = _split_bf16(a)
    b_hi, b_lo = _split_bf16(b)
    dot = functools.partial(jnp.dot, preferred_element_type=F32)
    return dot(a_hi, b_hi) + (dot(a_hi, b_lo) + dot(a_lo, b_hi))


def _params(*semantics):
    return pltpu.CompilerParams(dimension_semantics=semantics,
                                vmem_limit_bytes=VMEM_LIMIT_BYTES)


def _full(shape):
    return pl.BlockSpec(shape, lambda *_: (0,) * len(shape))


def _page_specs(layer, first_page, pages_per_step, step_of):
    def spec(r):
        def index_map(*args):
            page = first_page + step_of(*args[:-1]) * pages_per_step + r
            return (layer, args[-1][page], 0, 0, 0)
        return pl.BlockSpec((1, 1, N_HEADS, PAGE_SIZE, HEAD_DIM), index_map)
    return [spec(r) for r in range(pages_per_step)]


def _kmean_spec(pages_per_step, step_of):
    blocks = pages_per_step // PAGES_PER_BLOCK
    return pl.BlockSpec((1, blocks, N_HEADS, HEAD_DIM),
                        lambda *args: (step_of(*args[:-1]), 0, 0, 0))


def _kmean_shape(n_steps, pages_per_step):
    return jax.ShapeDtypeStruct(
        (n_steps, pages_per_step // PAGES_PER_BLOCK, N_HEADS, HEAD_DIM), F32)


def _mean_block(page_refs, kmean_ref, j, zero):
    tot = zero + jnp.sum(page_refs[j * PAGES_PER_BLOCK][0, 0], axis=1)
    for r in range(1, PAGES_PER_BLOCK):
        tot = tot + jnp.sum(page_refs[j * PAGES_PER_BLOCK + r][0, 0], axis=1)
    mean = tot / MOBA_BLOCK
    kmean_ref[0, j] = mean
    return mean


def _zero_after(x):
    bits = lax.bitcast_convert_type(x, jnp.uint32)
    return lax.bitcast_convert_type((bits >> 31) >> 1, F32)


def _dot_with_page_means(x, x_ref, w_ref, n_chunks, page_refs, kmean_ref):
    n = w_ref.shape[1] // n_chunks
    n_blocks = len(page_refs) // PAGES_PER_BLOCK
    per_gap = -(-n_blocks // max(n_chunks - 1, 1))
    x_ref[...] = x
    start = _zero_after(x[0:N_HEADS, 0:HEAD_DIM].astype(F32))
    outs, j = [], 0
    for c in range(n_chunks):
        outs.append(jnp.dot(x_ref[...], w_ref[:, c * n:(c + 1) * n],
                            preferred_element_type=F32))
        tie = None
        while j < min(n_blocks, (c + 1) * per_gap):
            mean = _mean_block(page_refs, kmean_ref, j, start)
            tie = mean if tie is None else tie + mean
            j += 1
        if tie is not None:
            zero = _zero_after(jnp.concatenate([tie, tie], axis=0)).astype(x_ref.dtype)
            x_ref[0:2 * N_HEADS, 0:HEAD_DIM] = x_ref[0:2 * N_HEADS, 0:HEAD_DIM] + zero
    return outs


def _mix_prompt_kernel(*refs, n_pages):
    x_ref, g_ref, win_ref, wc_ref = refs[1:5]
    page_refs = refs[5:5 + n_pages]
    a_ref, st_ref, kmean_ref, uext_ref, h_ref = refs[5 + n_pages:]
    s = pl.program_id(1)
    tm = x_ref.shape[1]

    @pl.when(s == 0)
    def _():
        uext_ref[0:CONV_PAD, :] = jnp.zeros((CONV_PAD, D_MODEL), F32)

    h = _rms(x_ref[0], g_ref[...]).astype(BF16)
    parts = _dot_with_page_means(h, h_ref, win_ref, 6, page_refs, kmean_ref)
    b, c, xin = [jnp.concatenate(parts[2 * i:2 * i + 2], axis=1) for i in range(3)]
    u = c * xin
    uext_ref[CONV_PAD:CONV_PAD + tm, :] = u
    um1 = uext_ref[CONV_PAD - 1:CONV_PAD - 1 + tm, :]
    um2 = uext_ref[CONV_PAD - 2:CONV_PAD - 2 + tm, :]
    conv = wc_ref[0:1, :] * um2 + wc_ref[1:2, :] * um1 + wc_ref[2:3, :] * u
    a_ref[0] = (b * conv).astype(BF16)
    tail = u[tm - 2:tm, :]
    uext_ref[CONV_PAD - 2:CONV_PAD, :] = tail

    @pl.when(s == pl.num_programs(1) - 1)
    def _():
        st_ref[0] = tail


def _mix_prompt(x, g, w_in, w_conv, tm, pages):
    bsz, seq, _ = x.shape
    n_s = seq // tm
    step_of = lambda b, s: b * n_s + s
    return pl.pallas_call(
        functools.partial(_mix_prompt_kernel, n_pages=pages.per_step),
        grid_spec=pltpu.PrefetchScalarGridSpec(
            num_scalar_prefetch=1,
            grid=(bsz, n_s),
            in_specs=[
                pl.BlockSpec((1, tm, D_MODEL), lambda b, s, pt: (b, s, 0)),
                _full((1, D_MODEL)),
                _full((D_MODEL, 3 * D_MODEL)),
                _full((3, D_MODEL)),
            ] + _page_specs(pages.layer, pages.first, pages.per_step, step_of),
            out_specs=[
                pl.BlockSpec((1, tm, D_MODEL), lambda b, s, pt: (b, s, 0)),
                pl.BlockSpec((1, 2, D_MODEL), lambda b, s, pt: (b, 0, 0)),
                _kmean_spec(pages.per_step, step_of),
            ],
            scratch_shapes=[pltpu.VMEM((CONV_PAD + tm, D_MODEL), F32),
                            pltpu.VMEM((tm, D_MODEL), BF16)],
        ),
        out_shape=[
            jax.ShapeDtypeStruct((bsz, seq, D_MODEL), BF16),
            jax.ShapeDtypeStruct((bsz, 2, D_MODEL), F32),
            _kmean_shape(bsz * n_s, pages.per_step),
        ],
        compiler_params=_params("arbitrary", "arbitrary"),
        name="mix_prompt",
    )(pages.table, x, g, w_in, w_conv, *([pages.cache] * pages.per_step))


def _mix_sample_kernel(x_ref, st_ref, g_ref, win_ref, wc_ref, a_ref, nst_ref):
    h = _rms(x_ref[...], g_ref[...]).astype(BF16)
    bcx = jnp.dot(h, win_ref[...], preferred_element_type=F32)
    b = bcx[:, :D_MODEL]
    u = bcx[:, D_MODEL:2 * D_MODEL] * bcx[:, 2 * D_MODEL:]
    um2 = st_ref[0]
    um1 = st_ref[1]
    conv = wc_ref[0:1, :] * um2 + wc_ref[1:2, :] * um1 + wc_ref[2:3, :] * u
    a_ref[...] = (b * conv).astype(BF16)
    nst_ref[0] = um1
    nst_ref[1] = u


def _mix_sample(x, state_t, g, w_in, w_conv):
    n = x.shape[0]
    return pl.pallas_call(
        _mix_sample_kernel,
        grid=(1,),
        in_specs=[
            _full((n, D_MODEL)),
            _full((2, n, D_MODEL)),
            _full((1, D_MODEL)),
            _full((D_MODEL, 3 * D_MODEL)),
            _full((3, D_MODEL)),
        ],
        out_specs=[_full((n, D_MODEL)), _full((2, n, D_MODEL))],
        out_shape=[
            jax.ShapeDtypeStruct((n, D_MODEL), BF16),
            jax.ShapeDtypeStruct((2, n, D_MODEL), F32),
        ],
        compiler_params=_params("arbitrary"),
        name="mix_sample",
    )(x, state_t, g, w_in, w_conv)


def _post_kernel(*refs, final_norm, n_pages):
    if n_pages:
        refs = refs[1:]
    res_ref, a_ref, wpre_ref, g_ref, wgu_ref, wd_ref, gf_ref = refs[:7]
    y_ref = refs[7 + n_pages]
    kmean_ref = refs[8 + n_pages] if n_pages else None
    h_ref = refs[-1]
    tm = res_ref.shape[0]
    n_groups = 2 if tm >= 2 * POST_GROUP_ROWS else 1
    rows = [slice(r * tm // n_groups, (r + 1) * tm // n_groups) for r in range(n_groups)]
    y1 = [res_ref[r, :] + jnp.dot(a_ref[r, :], wpre_ref[...], preferred_element_type=F32)
          for r in rows]
    h = [_rms(y, g_ref[...]).astype(BF16) for y in y1]
    gu = []
    for i, r in enumerate(rows):
        pages = refs[7:7 + n_pages] if i == 0 else ()
        gu.append(_dot_with_page_means(h[i], h_ref.at[r, :], wgu_ref, 2, pages, kmean_ref))
    act = [(gate * (1.0 / (1.0 + jnp.exp(-gate))) * up).astype(BF16) for gate, up in gu]
    for i, r in enumerate(rows):
        y2 = y1[i] + jnp.dot(act[i], wd_ref[...], preferred_element_type=F32)
        if final_norm:
            y2 = _rms(y2, gf_ref[...])
        y_ref[r, :] = y2


def _post(res, a, w_pre, g_ffn, w_gu, w_down, g_final, tm, final_norm, pages=None):
    m = res.shape[0]
    d_ff = w_down.shape[0]
    tile = pl.BlockSpec((tm, D_MODEL), lambda i, *_: (i, 0))
    in_specs = [
        tile, tile,
        _full((D_MODEL, D_MODEL)),
        _full((1, D_MODEL)),
        _full((D_MODEL, 2 * d_ff)),
        _full((d_ff, D_MODEL)),
        _full((1, D_MODEL)),
    ]
    out_specs = [tile]
    out_shape = [jax.ShapeDtypeStruct((m, D_MODEL), F32)]
    operands = [res, a, w_pre, g_ffn, w_gu, w_down, g_final]
    n_pages = 0
    if pages is not None:
        n_pages = pages.per_step
        step_of = lambda i: i
        in_specs += _page_specs(pages.layer, pages.first, n_pages, step_of)
        out_specs.append(_kmean_spec(n_pages, step_of))
        out_shape.append(_kmean_shape(m // tm, n_pages))
        operands = [pages.table] + operands + [pages.cache] * n_pages
    out = pl.pallas_call(
        functools.partial(_post_kernel, final_norm=final_norm, n_pages=n_pages),
        grid_spec=pltpu.PrefetchScalarGridSpec(
            num_scalar_prefetch=1 if n_pages else 0,
            grid=(m // tm,),
            in_specs=in_specs,
            out_specs=out_specs,
            scratch_shapes=[pltpu.VMEM((tm, D_MODEL), BF16)],
        ),
        out_shape=out_shape,
        compiler_params=_params("arbitrary"),
        name="post_final" if final_norm else "post",
    )(*operands)
    return out if n_pages else out[0]


def _qkv_prompt_kernel(x_ref, g_ref, w_ref, cos_ref, slo_ref, shi_ref,
                       kp_ref, vp_ref, qt_ref, kb_ref, vt_ref, km_ref):
    tm = x_ref.shape[1]
    h = _rms(x_ref[0], g_ref[...]).astype(BF16)
    qkv = jnp.dot(h, w_ref[...], preferred_element_type=F32)
    cos, slo, shi = cos_ref[...], slo_ref[...], shi_ref[...]
    for hd in range(N_HEADS):
        lo = hd * HEAD_DIM
        q = _rope(qkv[:, lo:lo + HEAD_DIM], cos, slo, shi)
        k = _rope(qkv[:, D_MODEL + lo:D_MODEL + lo + HEAD_DIM], cos, slo, shi)
        v = qkv[:, 2 * D_MODEL + lo:2 * D_MODEL + lo + HEAD_DIM]
        for p in range(tm // PAGE_SIZE):
            kp_ref[0, p, hd] = k[p * PAGE_SIZE:(p + 1) * PAGE_SIZE]
            vp_ref[0, p, hd] = v[p * PAGE_SIZE:(p + 1) * PAGE_SIZE]
        for j in range(tm // MOBA_BLOCK):
            rows = slice(j * MOBA_BLOCK, (j + 1) * MOBA_BLOCK)
            qt_ref[0, hd, j] = q[rows].T
            kb_ref[0, hd, j] = k[rows].astype(BF16)
            vt_ref[0, hd, j, :HEAD_DIM] = v[rows].T.astype(BF16)
            vt_ref[0, hd, j, HEAD_DIM:] = jnp.ones((VT_ONES_ROWS, MOBA_BLOCK), BF16)
            km_ref[0, j, :, lo:lo + HEAD_DIM] = jnp.mean(k[rows], axis=0, keepdims=True)


def _qkv_prompt(x, g, w_qkv, cos, slo, shi, tm):
    bsz, seq, _ = x.shape
    n_pages, n_blk = seq // PAGE_SIZE, seq // MOBA_BLOCK
    tp, tb = tm // PAGE_SIZE, tm // MOBA_BLOCK
    tab = pl.BlockSpec((tm, HEAD_DIM), lambda b, s: (s, 0))
    by_page = lambda b, s: (b, s, 0, 0, 0)
    by_block = lambda b, s: (b, 0, s, 0, 0)
    return pl.pallas_call(
        _qkv_prompt_kernel,
        grid=(bsz, seq // tm),
        in_specs=[
            pl.BlockSpec((1, tm, D_MODEL), lambda b, s: (b, s, 0)),
            _full((1, D_MODEL)),
            _full((D_MODEL, 3 * D_MODEL)),
            tab, tab, tab,
        ],
        out_specs=[
            pl.BlockSpec((1, tp, N_HEADS, PAGE_SIZE, HEAD_DIM), by_page),
            pl.BlockSpec((1, tp, N_HEADS, PAGE_SIZE, HEAD_DIM), by_page),
            pl.BlockSpec((1, N_HEADS, tb, HEAD_DIM, MOBA_BLOCK), by_block),
            pl.BlockSpec((1, N_HEADS, tb, MOBA_BLOCK, HEAD_DIM), by_block),
            pl.BlockSpec((1, N_HEADS, tb, HEAD_DIM + VT_ONES_ROWS, MOBA_BLOCK), by_block),
            pl.BlockSpec((1, tb, 1, D_MODEL), lambda b, s: (b, s, 0, 0)),
        ],
        out_shape=[
            jax.ShapeDtypeStruct((bsz, n_pages, N_HEADS, PAGE_SIZE, HEAD_DIM), F32),
            jax.ShapeDtypeStruct((bsz, n_pages, N_HEADS, PAGE_SIZE, HEAD_DIM), F32),
            jax.ShapeDtypeStruct((bsz, N_HEADS, n_blk, HEAD_DIM, MOBA_BLOCK), F32),
            jax.ShapeDtypeStruct((bsz, N_HEADS, n_blk, MOBA_BLOCK, HEAD_DIM), BF16),
            jax.ShapeDtypeStruct((bsz, N_HEADS, n_blk, HEAD_DIM + VT_ONES_ROWS, MOBA_BLOCK), BF16),
            jax.ShapeDtypeStruct((bsz, n_blk, 1, D_MODEL), F32),
        ],
        compiler_params=_params("arbitrary", "arbitrary"),
        name="qkv_prompt",
    )(x, g, w_qkv, cos, slo, shi)


def _qkv_sample_kernel(x_ref, g_ref, w_ref, cos_ref, slo_ref, shi_ref, q_ref, k_ref, v_ref):
    h = _rms(x_ref[...], g_ref[...]).astype(BF16)
    qkv = jnp.dot(h, w_ref[...], preferred_element_type=F32)
    cos, slo, shi = cos_ref[...], slo_ref[...], shi_ref[...]
    for hd in range(N_HEADS):
        lo = hd * HEAD_DIM
        q_ref[:, lo:lo + HEAD_DIM] = _rope(qkv[:, lo:lo + HEAD_DIM], cos, slo, shi)
        k_ref[:, lo:lo + HEAD_DIM] = _rope(
            qkv[:, D_MODEL + lo:D_MODEL + lo + HEAD_DIM], cos, slo, shi)
    v_ref[...] = qkv[:, 2 * D_MODEL:]


def _qkv_sample(x, g, w_qkv, cos, slo, shi):
    n = x.shape[0]
    out = jax.ShapeDtypeStruct((n, D_MODEL), F32)
    return pl.pallas_call(
        _qkv_sample_kernel,
        grid=(1,),
        in_specs=[
            _full((n, D_MODEL)),
            _full((1, D_MODEL)),
            _full((D_MODEL, 3 * D_MODEL)),
            _full((n, HEAD_DIM)), _full((n, HEAD_DIM)), _full((n, HEAD_DIM)),
        ],
        out_specs=[_full((n, D_MODEL))] * 3,
        out_shape=[out, out, out],
        compiler_params=_params("arbitrary"),
        name="qkv_sample",
    )(x, g, w_qkv, cos, slo, shi)


def _attn_prompt_kernel(*refs, n_pages):
    qt_ref, k_ref, vt_ref, km_ref = refs[1:5]
    page_refs = refs[5:5 + n_pages]
    (o_ref, kmean_ref,
     bias_ref, qs_ref, m_ref, mb_ref, alpha_ref, p_ref, s_ref, acc_ref) = refs[5 + n_pages:]
    for j in range(n_pages // PAGES_PER_BLOCK):
        _mean_block(page_refs, kmean_ref, j, jnp.zeros((N_HEADS, HEAD_DIM), F32))
    qb = pl.program_id(2)
    n_heads = qt_ref.shape[1]
    n_blk = km_ref.shape[2]
    tq = qt_ref.shape[4]
    heads = range(n_heads)
    blk = lax.broadcasted_iota(jnp.int32, (n_blk, tq), 0)
    past = blk < qb
    key = lax.broadcasted_iota(jnp.int32, (tq, tq), 0)
    qry = lax.broadcasted_iota(jnp.int32, (tq, tq), 1)
    causal = jnp.where(key <= qry, 0.0, NEG_INF)

    def score_stage(j, slot):
        scores = [jnp.dot(k_ref[0, h, j], qs_ref[h], preferred_element_type=F32)
                  for h in heads]
        for h in heads:
            s = scores[h] + bias_ref[h, pl.ds(j, 1), :]
            s_ref[slot, h] = s
            mb_ref[slot, h] = jnp.max(s, axis=0, keepdims=True)

    def value_stage(j, slot):
        pvs = [jnp.dot(vt_ref[0, h, j], p_ref[slot, h], preferred_element_type=F32)
               for h in heads]
        for h in heads:
            acc_ref[h] = alpha_ref[slot, h] * acc_ref[h] + pvs[h]

    gates = [_dot_3pass(km_ref[0, h], qt_ref[0, h, 0]) for h in heads]
    for h in heads:
        qs_ref[h] = (qt_ref[0, h, 0] * (ATTN_SCALE * LOG2_E)).astype(BF16)
    own = [jnp.dot(k_ref[0, h, qb], qs_ref[h], preferred_element_type=F32) for h in heads]
    for h in heads:
        gate = jnp.where(past, gates[h], NEG_INF)
        rank = jnp.zeros((n_blk, tq), F32)
        for jp in range(n_blk):
            row = gate[jp:jp + 1, :]
            tie = jnp.where(blk > jp, 1.0, 0.0)
            rank = rank + jnp.where(row > gate, 1.0, jnp.where(row == gate, tie, 0.0))
        bias_ref[h] = jnp.where(rank < MOBA_TOPK, jnp.where(past, 0.0, NEG_INF), NEG_INF)
    for h in heads:
        s = own[h] + causal
        m = jnp.max(s, axis=0, keepdims=True)
        m_ref[h] = m
        alpha_ref[1, h] = jnp.ones((1, tq), F32)
        p_ref[1, h] = jnp.exp2(s - m).astype(BF16)
        acc_ref[h] = jnp.zeros(acc_ref.shape[1:], F32)
    score_stage(0, 0)

    def trip(j, cur):
        nxt = 1 - cur
        score_stage(jnp.minimum(j + 1, n_blk - 1), nxt)
        for h in heads:
            m_old = m_ref[h]
            m_new = jnp.maximum(m_old, mb_ref[cur, h])
            m_ref[h] = m_new
            alpha_ref[cur, h] = jnp.exp2(m_old - m_new)
            p_ref[cur, h] = jnp.exp2(s_ref[cur, h] - m_new).astype(BF16)
        value_stage(jnp.where(j == 0, qb, j - 1), nxt)

    def body(i, carry):
        trip(2 * i, 0)

        @pl.when(2 * i + 1 < qb)
        def _():
            trip(2 * i + 1, 1)

        return carry

    lax.fori_loop(0, (qb + 1) // 2, body, 0)
    value_stage(jnp.where(qb == 0, qb, qb - 1), lax.rem(qb + 1, 2))
    for h in heads:
        o = acc_ref[h, :HEAD_DIM] * (1.0 / acc_ref[h, HEAD_DIM:HEAD_DIM + 1])
        o_ref[0, :, h * HEAD_DIM:(h + 1) * HEAD_DIM] = o.T.astype(BF16)


def _attn_prompt(qt, kb, vt, km, pages):
    bsz, _, n_blk, _, tq = qt.shape
    vt_rows = vt.shape[3]
    hps = ATTN_HEADS_PER_STEP
    n_hg = N_HEADS // hps
    step_of = lambda b, h, i: (b * n_hg + h) * n_blk + i
    return pl.pallas_call(
        functools.partial(_attn_prompt_kernel, n_pages=pages.per_step),
        grid_spec=pltpu.PrefetchScalarGridSpec(
            num_scalar_prefetch=1,
            grid=(bsz, n_hg, n_blk),
            in_specs=[
                pl.BlockSpec((1, hps, 1, HEAD_DIM, tq), lambda b, h, i, pt: (b, h, i, 0, 0)),
                pl.BlockSpec((1, hps, n_blk, tq, HEAD_DIM), lambda b, h, i, pt: (b, h, 0, 0, 0)),
                pl.BlockSpec((1, hps, n_blk, vt_rows, tq), lambda b, h, i, pt: (b, h, 0, 0, 0)),
                pl.BlockSpec((1, hps, n_blk, HEAD_DIM), lambda b, h, i, pt: (b, h, 0, 0)),
            ] + _page_specs(pages.layer, pages.first, pages.per_step, step_of),
            out_specs=[
                pl.BlockSpec((1, tq, hps * HEAD_DIM), lambda b, h, i, pt: (b, i, h)),
                _kmean_spec(pages.per_step, step_of),
            ],
            scratch_shapes=[
                pltpu.VMEM((hps, n_blk, tq), F32),
                pltpu.VMEM((hps, HEAD_DIM, tq), BF16),
                pltpu.VMEM((hps, 1, tq), F32),
                pltpu.VMEM((2, hps, 1, tq), F32),
                pltpu.VMEM((2, hps, 1, tq), F32),
                pltpu.VMEM((2, hps, tq, tq), BF16),
                pltpu.VMEM((2, hps, tq, tq), F32),
                pltpu.VMEM((hps, vt_rows, tq), F32),
            ],
        ),
        out_shape=[
            jax.ShapeDtypeStruct((bsz, n_blk * tq, D_MODEL), BF16),
            _kmean_shape(bsz * n_hg * n_blk, pages.per_step),
        ],
        compiler_params=_params("arbitrary", "arbitrary", "arbitrary"),
        name="attn_prompt",
    )(pages.table, qt, kb, vt, km, *([pages.cache] * pages.per_step))


def _select_sample_kernel(q_ref, km_ref, sel_ref):
    n_blk = km_ref.shape[1]
    for b in range(q_ref.shape[0]):
        prod = km_ref[b] * q_ref[b]
        gate = jnp.concatenate(
            [jnp.sum(prod[:, h * HEAD_DIM:(h + 1) * HEAD_DIM], axis=1, keepdims=True)
             for h in range(N_HEADS)], axis=1)
        blk = lax.broadcasted_iota(jnp.int32, gate.shape, 0)
        picks = []
        for _ in range(MOBA_TOPK):
            best = jnp.max(gate, axis=0, keepdims=True)
            idx = jnp.min(jnp.where(gate == best, blk, n_blk), axis=0, keepdims=True)
            picks.append(idx)
            gate = jnp.where(blk == idx, NEG_INF, gate)
        sel_ref[b] = jnp.concatenate(picks, axis=0)


def _select_sample(q, kmean):
    n_seq, n_blk = kmean.shape[0], kmean.shape[1]
    per_step = SELECT_SEQS_PER_STEP if n_seq % SELECT_SEQS_PER_STEP == 0 else 1
    return pl.pallas_call(
        _select_sample_kernel,
        grid=(n_seq // per_step,),
        in_specs=[
            pl.BlockSpec((per_step, 1, D_MODEL), lambda b: (b, 0, 0)),
            pl.BlockSpec((per_step, n_blk, D_MODEL), lambda b: (b, 0, 0)),
        ],
        out_specs=pl.BlockSpec((per_step, MOBA_TOPK, N_HEADS), lambda b: (b, 0, 0)),
        out_shape=jax.ShapeDtypeStruct((n_seq, MOBA_TOPK, N_HEADS), jnp.int32),
        compiler_params=_params("arbitrary"),
        name="select_sample",
    )(q.reshape(n_seq, 1, D_MODEL), kmean.reshape(n_seq, n_blk, D_MODEL))


def _attn_sample_kernel(phys_ref, q_ref, kn_ref, vn_ref, *refs):
    del phys_ref
    n_sel_pages = (len(refs) - 1) // 2
    k_refs, v_refs, o_ref = refs[:n_sel_pages], refs[n_sel_pages:-1], refs[-1]
    q = q_ref[0, 0]
    s_new = jnp.sum(q * kn_ref[0, 0], axis=1, keepdims=True) * ATTN_SCALE
    scores = [jnp.sum(k_ref[0, 0, 0] * q, axis=1, keepdims=True) * ATTN_SCALE
              for k_ref in k_refs]
    m = s_new
    for s in scores:
        m = jnp.maximum(m, jnp.max(s, axis=0, keepdims=True))
    p_new = jnp.exp(s_new - m)
    l = p_new
    acc = p_new * vn_ref[0, 0]
    for s, v_ref in zip(scores, v_refs):
        p = jnp.exp(s - m)
        l = l + jnp.sum(p, axis=0, keepdims=True)
        acc = acc + jnp.sum(p * v_ref[0, 0, 0], axis=0, keepdims=True)
    o_ref[0, 0] = (acc * (1.0 / l)).astype(o_ref.dtype)


def _attn_sample(q, kn, vn, cache_k, cache_v, phys, layer):
    n_seq = q.shape[0]
    n_sel = MOBA_TOPK * PAGES_PER_BLOCK
    page_block = (1, 1, 1, PAGE_SIZE, HEAD_DIM)

    def page_spec(i):
        return pl.BlockSpec(
            page_block, lambda b, h, ph: (layer, ph[(b * N_HEADS + h) * n_sel + i], h, 0, 0))

    row = pl.BlockSpec((1, 1, 1, HEAD_DIM), lambda b, h, ph: (b, h, 0, 0))
    pages = [page_spec(i) for i in range(n_sel)]
    return pl.pallas_call(
        _attn_sample_kernel,
        grid_spec=pltpu.PrefetchScalarGridSpec(
            num_scalar_prefetch=1,
            grid=(n_seq, N_HEADS),
            in_specs=[row, row, row] + pages + pages,
            out_specs=row,
        ),
        out_shape=jax.ShapeDtypeStruct((n_seq, N_HEADS, 1, HEAD_DIM), BF16),
        compiler_params=_params("arbitrary", "arbitrary"),
        name="attn_sample",
    )(phys, q, kn, vn, *([cache_k] * n_sel), *([cache_v] * n_sel))


class _Pages(NamedTuple):
    cache: jax.Array
    table: jax.Array
    layer: int
    first: int
    per_step: int


class _Plan(NamedTuple):
    tm_mix: int
    tm_post: int
    tm_qkv: int
    quarter: int

    def pages(self, cache, table, layer, idx, n_steps):
        per_step = self.quarter // n_steps
        assert per_step * n_steps == self.quarter and per_step % PAGES_PER_BLOCK == 0
        return _Pages(cache, table, layer, idx * self.quarter, per_step)


def _plan(n_cache_pages):
    assert n_cache_pages % 4 == 0
    return _Plan(tm_mix=512, tm_post=512, tm_qkv=512, quarter=n_cache_pages // 4)


def kernel(x_prompt, x_sample, state_conv, cache_k, cache_v, page_table,
           norm_mix, norm_ffn, norm_final, w_conv_in, w_conv, w_conv_out,
           w_qkv, w_o, w_gate_up, w_down):
    bsz, seq, _ = x_prompt.shape
    n_seq, n_pages = page_table.shape
    n_rows = bsz * seq
    plan = _plan(n_seq * n_pages)
    table = page_table.reshape(-1)
    row = lambda v: v.reshape(1, D_MODEL)
    w_in_b = w_conv_in[0].astype(BF16)
    w_out_b = w_conv_out[0].astype(BF16)
    w_qkv_b = w_qkv[0].astype(BF16)
    w_o_b = w_o[0].astype(BF16)
    w_gu_b = [w.astype(BF16) for w in w_gate_up]
    w_down_b = [w.astype(BF16) for w in w_down]
    g_final = row(norm_final)

    a_p, conv_p, kmean_0 = _mix_prompt(
        x_prompt, row(norm_mix[0]), w_in_b, w_conv[0], plan.tm_mix,
        plan.pages(cache_k, table, 0, 0, n_rows // plan.tm_mix))
    y_p, kmean_1 = _post(
        x_prompt.reshape(n_rows, D_MODEL), a_p.reshape(n_rows, D_MODEL),
        w_out_b, row(norm_ffn[0]), w_gu_b[0], w_down_b[0], g_final, plan.tm_post, False,
        plan.pages(cache_k, table, 0, 1, n_rows // plan.tm_post))

    xs = x_sample.reshape(n_seq, D_MODEL)
    a_s, conv_s = _mix_sample(xs, state_conv[0].transpose(1, 0, 2), row(norm_mix[0]),
                              w_in_b, w_conv[0])
    y_s = _post(xs, a_s, w_out_b, row(norm_ffn[0]), w_gu_b[0], w_down_b[0], g_final,
                n_seq, False)

    cos, slo, shi = _rope_tables(jnp.arange(seq))
    k_pages, v_pages, qt, kb, vt, km = _qkv_prompt(
        y_p.reshape(bsz, seq, D_MODEL), row(norm_mix[1]), w_qkv_b, cos, slo, shi, plan.tm_qkv)
    n_blk = seq // MOBA_BLOCK
    km_h = km.reshape(bsz, n_blk, N_HEADS, HEAD_DIM).transpose(0, 2, 1, 3)
    attn_p, kmean_2 = _attn_prompt(
        qt, kb, vt, km_h,
        plan.pages(cache_k, table, 0, 2, bsz * (N_HEADS // ATTN_HEADS_PER_STEP) * n_blk))
    y_p, kmean_3 = _post(
        y_p, attn_p.reshape(n_rows, D_MODEL), w_o_b, row(norm_ffn[1]),
        w_gu_b[1], w_down_b[1], g_final, plan.tm_post, True,
        plan.pages(cache_k, table, 0, 3, n_rows // plan.tm_post))

    cos_s, slo_s, shi_s = _rope_tables(PAST_LEN + jnp.arange(1))
    bc = lambda t: jnp.broadcast_to(t, (n_seq, HEAD_DIM))
    q_s, k_s, v_s = _qkv_sample(y_s, row(norm_mix[1]), w_qkv_b, bc(cos_s), bc(slo_s), bc(shi_s))
    kmean_s = jnp.concatenate(
        [part.reshape(-1, N_HEADS, HEAD_DIM) for part in (kmean_0, kmean_1, kmean_2, kmean_3)],
        axis=0).reshape(n_seq, n_pages // PAGES_PER_BLOCK, N_HEADS, HEAD_DIM)
    sel = _select_sample(q_s, kmean_s)
    heads = lambda t: t.reshape(n_seq, N_HEADS, 1, HEAD_DIM)
    logical = (sel.transpose(0, 2, 1)[..., None] * PAGES_PER_BLOCK
               + jnp.arange(PAGES_PER_BLOCK, dtype=jnp.int32))
    phys = jnp.take_along_axis(page_table, logical.reshape(n_seq, -1), axis=1)
    attn_s = _attn_sample(heads(q_s), heads(k_s), heads(v_s), cache_k, cache_v,
                          phys.reshape(-1), 0)
    y_s = _post(y_s, attn_s.reshape(n_seq, D_MODEL), w_o_b, row(norm_ffn[1]),
                w_gu_b[1], w_down_b[1], g_final, n_seq, True)

    return (y_p.reshape(bsz, seq, D_MODEL),
            y_s.reshape(n_seq, 1, D_MODEL),
            conv_p[None],
            conv_s.transpose(1, 0, 2)[None],
            k_pages[None],
            v_pages[None],
            heads(k_s)[None],
            heads(v_s)[None])
```
